```python
import functools
import jax, jax.numpy as jnp
from jax import lax
import numpy as np

D_MODEL = 1024
BATCH = 2
SEQ = 8192
DEPTH = 4
DEC_BATCH = 128
DEC_SEQ = 4
PAST_LEN = 2048
PAGE_SIZE = 128

N_META = 16
N_A_LAYERS = DEPTH // 2
N_B_LAYERS = DEPTH - N_A_LAYERS
CONV_WIDTH = 3
N_HEADS = 16
HEAD_DIM = D_MODEL // N_HEADS
ATTN_DIM = N_HEADS * HEAD_DIM
D_FF = 4 * D_MODEL
Q_BLOCK = 128
RMS_EPS = 1e-6
FORGET_BIAS = 3.0
NEG_INF = -1e30

kernel_name = "yoco_shortconv_fox_decoder_step"


def rms_norm(x, g):
    xf = x.astype(jnp.float32)
    y = xf * lax.rsqrt(jnp.mean(xf * xf, axis=-1, keepdims=True) + RMS_EPS)
    return (y * g.astype(jnp.float32)).astype(x.dtype)


def squared_relu_mlp(xn, w1, w2):
    return jnp.square(jax.nn.relu(xn @ w1)) @ w2


def short_conv_mixer(xn, conv_state, w_in, conv_w, w_out):
    T = xn.shape[1]
    b, c, h = jnp.split(xn @ w_in, 3, axis=-1)
    u = c * h
    u_pad = jnp.concatenate([conv_state.astype(u.dtype), u], axis=1)
    conv = conv_w[CONV_WIDTH - 1] * u_pad[:, CONV_WIDTH - 1:]
    for j in range(CONV_WIDTH - 1):
        conv = conv + conv_w[j] * u_pad[:, j:j + T]
    y = (b * conv) @ w_out
    return y, u_pad[:, T:]


def shared_kv(h, norm_kv, w_kv, k_norm, w_f, b_f):
    N, T, _ = h.shape
    xn = rms_norm(h, norm_kv)
    k, v = jnp.split(xn @ w_kv, 2, axis=-1)
    k = rms_norm(k.reshape(N, T, N_HEADS, HEAD_DIM), k_norm)
    v = v.reshape(N, T, N_HEADS, HEAD_DIM)
    logf = jax.nn.log_sigmoid((xn @ w_f + b_f).astype(jnp.float32)).astype(h.dtype)
    return k, v, logf


def forget_attention(q, k, v, c_q, c_k, q_pos, k_pos):
    s = jnp.einsum('nqhd,nkhd->nhqk', q, k, preferred_element_type=jnp.float32) * (HEAD_DIM ** -0.5)
    s = s + (jnp.transpose(c_q, (0, 2, 1))[:, :, :, None] - jnp.transpose(c_k, (0, 2, 1))[:, :, None, :])
    causal = k_pos[None, :] <= q_pos[:, None]
    s = jnp.where(causal[None, None], s, NEG_INF)
    p = jax.nn.softmax(s, axis=-1)
    return jnp.einsum('nhqk,nkhd->nqhd', p.astype(v.dtype), v)


def prompt_prepare(k, v, logf):
    T = k.shape[1]
    pad = (-T) % Q_BLOCK
    k = jnp.pad(k, ((0, 0), (0, pad), (0, 0), (0, 0)))
    v = jnp.pad(v, ((0, 0), (0, pad), (0, 0), (0, 0)))
    c = jnp.cumsum(jnp.pad(logf.astype(jnp.float32), ((0, 0), (0, pad), (0, 0))), axis=1)
    return k, v, c


def prompt_attend(q, ctx):
    k, v, c = ctx
    N, T = q.shape[:2]
    Tp = k.shape[1]
    nb = Tp // Q_BLOCK
    qp = jnp.pad(q, ((0, 0), (0, Tp - T), (0, 0), (0, 0)))
    pos = jnp.arange(Tp, dtype=jnp.int32)
    qb = qp.reshape(N, nb, Q_BLOCK, N_HEADS, HEAD_DIM).swapaxes(0, 1)
    cb = c.reshape(N, nb, Q_BLOCK, N_HEADS).swapaxes(0, 1)
    pb = pos.reshape(nb, Q_BLOCK)

    def block(args):
        q_blk, c_blk, p_blk = args
        return forget_attention(q_blk, k, v, c_blk, c, p_blk, pos)

    out = lax.map(block, (qb, cb, pb))
    return out.swapaxes(0, 1).reshape(N, Tp, N_HEADS, HEAD_DIM)[:, :T]


def gather_pages(cache, page_table):
    g = cache[page_table]
    return g.reshape((page_table.shape[0], page_table.shape[1] * cache.shape[1]) + cache.shape[2:])


def sample_prepare(k, v, logf, past_k, past_v, past_logf):
    k_all = jnp.concatenate([past_k.astype(k.dtype), k], axis=1)
    v_all = jnp.concatenate([past_v.astype(v.dtype), v], axis=1)
    c_all = jnp.cumsum(jnp.concatenate([past_logf.astype(jnp.float32), logf.astype(jnp.float32)], axis=1), axis=1)
    return k_all, v_all, c_all


def sample_attend(q, ctx):
    k_all, v_all, c_all = ctx
    L = k_all.shape[1]
    Tq = q.shape[1]
    pos = jnp.arange(L, dtype=jnp.int32)
    return forget_attention(q, k_all, v_all, c_all[:, L - Tq:], c_all, pos[L - Tq:], pos)


def decoder_trunk(x, conv_state, prepare, attend, norm_mix, norm_mlp, conv_w_in, conv_w, conv_w_out,
                  norm_kv, w_kv, k_norm, w_f, b_f, w_q, q_norm, w_o, mlp_w1, mlp_w2):
    N, T, _ = x.shape
    conv_rows = []
    for layer in range(DEPTH):
        xn = rms_norm(x, norm_mix[layer])
        if layer < N_A_LAYERS:
            y, st = short_conv_mixer(xn, conv_state[layer], conv_w_in[layer], conv_w[layer], conv_w_out[layer])
            conv_rows.append(st)
        else:
            j = layer - N_A_LAYERS
            if j == 0:
                k, v, logf = shared_kv(x, norm_kv, w_kv, k_norm, w_f, b_f)
                ctx = prepare(k, v, logf)
            q = rms_norm((xn @ w_q[j]).reshape(N, T, N_HEADS, HEAD_DIM), q_norm[j])
            y = attend(q, ctx).reshape(N, T, ATTN_DIM) @ w_o[j]
        x = x + y
        x = x + squared_relu_mlp(rms_norm(x, norm_mlp[layer]), mlp_w1[layer], mlp_w2[layer])
    return x, jnp.stack(conv_rows), k, v, logf


def setup_inputs(seed: int = 0) -> dict:
    key = jax.random.key(seed)
    ks = jax.random.split(key, 26)
    f32 = jnp.float32
    n_pages = PAST_LEN // PAGE_SIZE
    n_used = DEC_BATCH * n_pages
    n_pool = (5 * n_used + 3) // 4

    def nrm(k, shape, scale):
        return scale * jax.random.normal(k, shape, f32)

    return {
        "x_prompt": nrm(ks[0], (BATCH, SEQ, D_MODEL), 1.0),
        "x_sample": nrm(ks[1], (DEC_BATCH, DEC_SEQ, D_MODEL), 1.0),
        "cache_k": nrm(ks[2], (n_pool, PAGE_SIZE, N_HEADS, HEAD_DIM), 1.0),
        "cache_v": nrm(ks[3], (n_pool, PAGE_SIZE, N_HEADS, HEAD_DIM), 1.0),
        "cache_logf": jax.nn.log_sigmoid(FORGET_BIAS + nrm(ks[4], (n_pool, PAGE_SIZE, N_HEADS), 1.0)),
        "state_conv": nrm(ks[5], (N_A_LAYERS, DEC_BATCH, CONV_WIDTH - 1, D_MODEL), 1.0),
        "page_table": jax.random.permutation(ks[6], n_pool)[:n_used].reshape(DEC_BATCH, n_pages).astype(jnp.int32),
        "meta_tokens": nrm(ks[7], (N_META, D_MODEL), 1.0),
        "norm_mix": 1.0 + nrm(ks[8], (DEPTH, D_MODEL), 0.02),
        "norm_mlp": 1.0 + nrm(ks[9], (DEPTH, D_MODEL), 0.02),
        "conv_w_in": nrm(ks[10], (N_A_LAYERS, D_MODEL, 3 * D_MODEL), D_MODEL ** -0.5),
        "conv_w": nrm(ks[11], (N_A_LAYERS, CONV_WIDTH, D_MODEL), CONV_WIDTH ** -0.5),
        "conv_w_out": nrm(ks[12], (N_A_LAYERS, D_MODEL, D_MODEL), D_MODEL ** -0.5),
        "norm_kv": 1.0 + nrm(ks[13], (D_MODEL,), 0.02),
        "w_kv": nrm(ks[14], (D_MODEL, 2 * ATTN_DIM), D_MODEL ** -0.5),
        "k_norm": 1.0 + nrm(ks[15], (HEAD_DIM,), 0.02),
        "w_f": nrm(ks[16], (D_MODEL, N_HEADS), 0.5 * D_MODEL ** -0.5),
        "b_f": FORGET_BIAS + nrm(ks[17], (N_HEADS,), 0.1),
        "w_q": nrm(ks[18], (N_B_LAYERS, D_MODEL, ATTN_DIM), D_MODEL ** -0.5),
        "q_norm": 1.0 + nrm(ks[19], (N_B_LAYERS, HEAD_DIM), 0.02),
        "w_o": nrm(ks[20], (N_B_LAYERS, ATTN_DIM, D_MODEL), ATTN_DIM ** -0.5),
        "mlp_w1": nrm(ks[21], (DEPTH, D_MODEL, D_FF), D_MODEL ** -0.5),
        "mlp_w2": nrm(ks[22], (DEPTH, D_FF, D_MODEL), D_FF ** -0.5),
    }


def reference(x_prompt, x_sample, cache_k, cache_v, cache_logf, state_conv, page_table, meta_tokens,
              norm_mix, norm_mlp, conv_w_in, conv_w, conv_w_out, norm_kv, w_kv, k_norm, w_f, b_f,
              w_q, q_norm, w_o, mlp_w1, mlp_w2):
    weights = (norm_mix, norm_mlp, conv_w_in, conv_w, conv_w_out, norm_kv, w_kv, k_norm, w_f, b_f,
               w_q, q_norm, w_o, mlp_w1, mlp_w2)

    nb = x_prompt.shape[0]
    meta = jnp.broadcast_to(meta_tokens[None].astype(x_prompt.dtype), (nb, N_META, D_MODEL))
    xp = jnp.concatenate([meta, x_prompt], axis=1)
    conv0 = jnp.zeros((N_A_LAYERS, nb, CONV_WIDTH - 1, D_MODEL), x_prompt.dtype)
    hp, conv_prompt, k_prompt, v_prompt, logf_prompt = decoder_trunk(
        xp, conv0, prompt_prepare, prompt_attend, *weights)
    y_prompt = hp[:, N_META:]

    past_k = gather_pages(cache_k, page_table)
    past_v = gather_pages(cache_v, page_table)
    past_logf = gather_pages(cache_logf, page_table)
    prep = functools.partial(sample_prepare, past_k=past_k, past_v=past_v, past_logf=past_logf)
    y_sample, conv_sample, k_sample, v_sample, logf_sample = decoder_trunk(
        x_sample, state_conv, prep, sample_attend, *weights)

    return (y_prompt, y_sample, k_prompt, v_prompt, logf_prompt, conv_prompt,
            k_sample, v_sample, logf_sample, conv_sample)
```

```python
import functools

import numpy as np
import jax
import jax.numpy as jnp
from jax import lax
from jax.experimental import pallas as pl
from jax.experimental.pallas import tpu as pltpu

N_HEADS = 16
HEAD_DIM = 64
N_META = 16
CONV_WIDTH = 3
RMS_EPS = 1e-6
NEG_INF = -1e30
LOG2E = 1.4426950408889634

LANES = 128
VMEM_LIMIT = 56 * 1024 * 1024
ROW_TILE = 384
Q_TILE = 256
K_TILE = 128
N_AUG = 6

f32 = jnp.float32
bf16 = jnp.bfloat16


def _cparams(sem):
    return pltpu.CompilerParams(dimension_semantics=sem, vmem_limit_bytes=VMEM_LIMIT)


def _dot(a, b):
    return jnp.dot(a, b, preferred_element_type=f32)


def _rms(x, g):
    ms = jnp.mean(x * x, axis=-1, keepdims=True)
    return x * lax.rsqrt(ms + RMS_EPS) * g


def _split3(x):
    hi = x.astype(bf16)
    r1 = x - hi.astype(f32)
    mid = r1.astype(bf16)
    lo = (r1 - mid.astype(f32)).astype(bf16)
    return hi, mid, lo


def _dot3(x, w):
    hi, mid, lo = _split3(x)
    return _dot(hi, w) + _dot(mid, w) + _dot(lo, w)


def _dot3_left(w, x):
    hi, mid, lo = _split3(x)
    return _dot(w, hi) + _dot(w, mid) + _dot(w, lo)


def _head_rms(y, seg, seg_t, gain):
    ssq = _dot((y * y).astype(bf16), seg)
    inv = lax.rsqrt(ssq * (1.0 / HEAD_DIM) + RMS_EPS)
    hi = inv.astype(bf16)
    lo = (inv - hi.astype(f32)).astype(bf16)
    inv_full = _dot(hi, seg_t) + _dot(lo, seg_t)
    return y * inv_full * gain


def _conv_kernel(*refs, tm, mode, st_tile, st_row, seq_len):
    if mode == "prompt":
        x_ref, g_ref, win_ref, cw_ref, wout_ref, o_ref, st_ref, ubuf = refs
    else:
        x_ref, s1_ref, s2_ref, g_ref, win_ref, cw_ref, wout_ref, o_ref, st_ref = refs
    x = x_ref[...]
    d = x.shape[1]
    xn = _rms(x, g_ref[...])
    bch = _dot(xn.astype(bf16), win_ref[...])
    b = bch[:, :d]
    u = bch[:, d:2 * d] * bch[:, 2 * d:]
    if mode == "prompt":
        i = pl.program_id(1)

        @pl.when(i == 0)
        def _():
            ubuf[0:8, :] = jnp.zeros((8, d), f32)

        @pl.when(i > 0)
        def _():
            ubuf[0:8, :] = ubuf[tm:tm + 8, :]

        ubuf[8:tm + 8, :] = u
        prev1 = ubuf[7:tm + 7, :]
        prev2 = ubuf[6:tm + 6, :]

        @pl.when(i == st_tile)
        def _():
            st_ref[...] = u[st_row:st_row + 8, :]
    else:
        t = lax.broadcasted_iota(jnp.int32, (tm, d), 0) % seq_len
        prev1 = jnp.where(t >= 1, pltpu.roll(u, 1, 0), s1_ref[...])
        prev2 = jnp.where(t >= 2, pltpu.roll(u, 2, 0), s2_ref[...])
        st_ref[...] = u
    cw = cw_ref[...]
    conv = cw[2:3] * u + cw[0:1] * prev2 + cw[1:2] * prev1
    y = _dot((b * conv).astype(bf16), wout_ref[...])
    o_ref[...] = x + y


def _conv_prompt(x, g, w_in, cw, w_out, layer, t_real):
    n, tp, d = x.shape
    tm = ROW_TILE
    assert tp % tm == 0 and (t_real - 2) % 8 == 6
    st_tile, st_off = divmod(t_real - 2, tm)
    st_row = st_off // 8 * 8
    kern = functools.partial(_conv_kernel, tm=tm, mode="prompt", st_tile=st_tile, st_row=st_row, seq_len=0)
    out, st = pl.pallas_call(
        kern,
        grid=(n, tp // tm),
        in_specs=[
            pl.BlockSpec((None, tm, d), lambda a, i: (a, i, 0)),
            pl.BlockSpec((None, 1, d), lambda a, i: (layer, 0, 0)),
            pl.BlockSpec((None, d, 3 * d), lambda a, i: (layer, 0, 0)),
            pl.BlockSpec((None, CONV_WIDTH, d), lambda a, i: (layer, 0, 0)),
            pl.BlockSpec((None, d, d), lambda a, i: (layer, 0, 0)),
        ],
        out_specs=[
            pl.BlockSpec((None, tm, d), lambda a, i: (a, i, 0)),
            pl.BlockSpec((None, 8, d), lambda a, i: (a, 0, 0)),
        ],
        out_shape=[jax.ShapeDtypeStruct((n, tp, d), f32), jax.ShapeDtypeStruct((n, 8, d), f32)],
        scratch_shapes=[pltpu.VMEM((tm + 8, d), f32)],
        compiler_params=_cparams(("arbitrary", "arbitrary")),
        name="conv_prompt",
    )(x, g, w_in, cw, w_out)
    return out, st[:, 6:8]


def _conv_sample(x, state, g, w_in, cw, w_out, layer, seq_len):
    m, d = x.shape
    nb = m // seq_len
    z = jnp.zeros((nb, seq_len - 1, d), f32)
    s1 = jnp.concatenate([state[:, 1:2], z], axis=1).reshape(m, d)
    s2 = jnp.concatenate([state[:, 0:1], state[:, 1:2], z[:, 1:]], axis=1).reshape(m, d)
    kern = functools.partial(_conv_kernel, tm=m, mode="sample", st_tile=0, st_row=0, seq_len=seq_len)
    full = lambda i: (0, 0)
    out, u = pl.pallas_call(
        kern,
        grid=(1,),
        in_specs=[
            pl.BlockSpec((m, d), full),
            pl.BlockSpec((m, d), full),
            pl.BlockSpec((m, d), full),
            pl.BlockSpec((None, 1, d), lambda i: (layer, 0, 0)),
            pl.BlockSpec((None, d, 3 * d), lambda i: (layer, 0, 0)),
            pl.BlockSpec((None, CONV_WIDTH, d), lambda i: (layer, 0, 0)),
            pl.BlockSpec((None, d, d), lambda i: (layer, 0, 0)),
        ],
        out_specs=[pl.BlockSpec((m, d), full), pl.BlockSpec((m, d), full)],
        out_shape=[jax.ShapeDtypeStruct((m, d), f32), jax.ShapeDtypeStruct((m, d), f32)],
        compiler_params=_cparams(("arbitrary",)),
        name="conv_sample",
    )(x, s1, s2, g, w_in, cw, w_out)
    return out, u.reshape(nb, seq_len, d)[:, seq_len - 2:]


def _mlp_kernel(*refs, has_proj, n_chunks):
    if has_proj:
        x_ref, a_ref, wa_ref, g_ref, w1_ref, w2_ref, o_ref = refs
    else:
        x_ref, g_ref, w1_ref, w2_ref, o_ref = refs
    x = x_ref[...]
    if has_proj:
        x = x + _dot(a_ref[...], wa_ref[...])
    xn = _rms(x, g_ref[...]).astype(bf16)
    ck = w1_ref.shape[1] // n_chunks
    y = x
    for c in range(n_chunks):
        h = _dot(xn, w1_ref[:, c * ck:(c + 1) * ck])
        h = jnp.square(jnp.maximum(h, 0.0)).astype(bf16)
        y = y + _dot(h, w2_ref[c * ck:(c + 1) * ck, :])
    o_ref[...] = y


def _mlp(x, g, w1, w2, layer, proj=None):
    m, d = x.shape
    dff = w1.shape[2]
    tm = ROW_TILE if m % ROW_TILE == 0 else m
    row = lambda i: (i, 0)
    in_specs = [pl.BlockSpec((tm, d), row)]
    args = [x]
    if proj is not None:
        a, wa, pl_idx = proj
        in_specs += [pl.BlockSpec((tm, a.shape[1]), row),
                     pl.BlockSpec((None, a.shape[1], d), lambda i: (pl_idx, 0, 0))]
        args += [a, wa]
    in_specs += [
        pl.BlockSpec((None, 1, d), lambda i: (layer, 0, 0)),
        pl.BlockSpec((None, d, dff), lambda i: (layer, 0, 0)),
        pl.BlockSpec((None, dff, d), lambda i: (layer, 0, 0)),
    ]
    args += [g, w1, w2]
    kern = functools.partial(_mlp_kernel, has_proj=proj is not None, n_chunks=4)
    return pl.pallas_call(
        kern,
        grid=(m // tm,),
        in_specs=in_specs,
        out_specs=pl.BlockSpec((tm, d), row),
        out_shape=jax.ShapeDtypeStruct((m, d), f32),
        compiler_params=_cparams(("arbitrary",)),
        name="mlp",
    )(*args)


def _aug_offset(h):
    return LANES * h + (1 - h % 2) * HEAD_DIM


def _kv_constants():
    att = N_HEADS * HEAD_DIM
    seg = np.zeros((att, LANES), np.float32)
    seg[np.arange(att), np.arange(att) // HEAD_DIM] = 1.0
    pneg = np.zeros((3, LANES, N_HEADS * LANES), np.float32)
    ones_row = np.zeros((1, N_HEADS * LANES), np.float32)
    kmask = np.zeros((1, N_HEADS * LANES), np.float32)
    for h in range(N_HEADS):
        a = _aug_offset(h)
        ones_row[0, a:a + 3] = 1.0
        for part in range(3):
            pneg[part, h, a + 3 + part] = -1.0
        k0 = LANES * h + (h % 2) * HEAD_DIM
        kmask[0, k0:k0 + HEAD_DIM] = 1.0
    return (jnp.asarray(seg, bf16), jnp.asarray(seg.T, bf16), jnp.asarray(pneg, bf16),
            jnp.asarray(ones_row), jnp.asarray(kmask))


def _kv_kernel(*refs, tm, tk, prompt):
    if prompt:
        (x_ref, g_ref, wk_ref, wv_ref, wf_ref, bf_ref, seg_ref, segt_ref, kg_ref,
         ltri_ref, pneg_ref, ones_ref, kmask_ref,
         k_ref, v_ref, lf_ref, c_ref, kp_ref, vt_ref, carry) = refs
    else:
        (x_ref, g_ref, wk_ref, wv_ref, wf_ref, bf_ref, seg_ref, segt_ref, kg_ref,
         k_ref, v_ref, lf_ref) = refs
    xn = _rms(x_ref[...], g_ref[...])
    xb = xn.astype(bf16)
    k = _head_rms(_dot(xb, wk_ref[...]), seg_ref[...], segt_ref[...], kg_ref[...])
    v = _dot(xb, wv_ref[...])
    z = _dot(xb, wf_ref[...]) + bf_ref[...]
    lane = lax.broadcasted_iota(jnp.int32, z.shape, 1)
    logf = jnp.where(lane < N_HEADS, jax.nn.log_sigmoid(z), 0.0)
    k_ref[...] = k
    v_ref[...] = v
    lf_ref[...] = logf
    if not prompt:
        return

    @pl.when(pl.program_id(1) == 0)
    def _():
        carry[...] = jnp.zeros(carry.shape, f32)

    c = _dot3_left(ltri_ref[...], logf) + carry[0:1, :]
    carry[...] = jnp.broadcast_to(c[tm - 1:tm, :], carry.shape)
    c2 = c * LOG2E
    c_ref[...] = c2
    hi, mid, lo = _split3(c2)
    aug = _dot(hi, pneg_ref[0]) + _dot(mid, pneg_ref[1]) + _dot(lo, pneg_ref[2]) + ones_ref[...]
    pairs = []
    for p in range(N_HEADS // 2):
        blk = k[:, p * LANES:(p + 1) * LANES]
        pairs += [blk, blk]
    k_rep = jnp.concatenate(pairs, axis=1)
    kp_ref[...] = jnp.where(kmask_ref[...] > 0.5, k_rep, aug).astype(bf16)
    vt = v.T.astype(bf16)
    for j in range(tm // tk):
        vt_ref[j] = vt[:, j * tk:(j + 1) * tk]


def _kv_side(x, g, wk, wv, wf, bfp, kgain, prompt):
    n, t, d = x.shape
    att = N_HEADS * HEAD_DIM
    tm = ROW_TILE if t % ROW_TILE == 0 else t
    tk = K_TILE
    seg, seg_t, pneg, ones_row, kmask = _kv_constants()
    const = lambda shape: pl.BlockSpec(shape, lambda a, i: (0,) * len(shape))
    row = lambda w: pl.BlockSpec((None, tm, w), lambda a, i: (a, i, 0))
    in_specs = [row(d), const((1, d)), const((d, att)), const((d, att)), const((d, LANES)),
                const((1, LANES)), const((att, LANES)), const((LANES, att)), const((1, att))]
    args = [x, g, wk, wv, wf, bfp, seg, seg_t, kgain]
    out_specs = [row(att), row(att), row(LANES)]
    out_shape = [jax.ShapeDtypeStruct((n, t, att), f32), jax.ShapeDtypeStruct((n, t, att), f32),
                 jax.ShapeDtypeStruct((n, t, LANES), f32)]
    scratch = []
    if prompt:
        ltri = jnp.asarray(np.tril(np.ones((tm, tm), np.float32)), bf16)
        in_specs += [const((tm, tm)), const((3, LANES, N_HEADS * LANES)),
                     const((1, N_HEADS * LANES)), const((1, N_HEADS * LANES))]
        args += [ltri, pneg, ones_row, kmask]
        out_specs += [row(LANES), row(N_HEADS * LANES),
                      pl.BlockSpec((None, tm // tk, att, tk), lambda a, i: (a, i, 0, 0))]
        out_shape += [jax.ShapeDtypeStruct((n, t, LANES), f32),
                      jax.ShapeDtypeStruct((n, t, N_HEADS * LANES), bf16),
                      jax.ShapeDtypeStruct((n, t // tk, att, tk), bf16)]
        scratch = [pltpu.VMEM((8, LANES), f32)]
    kern = functools.partial(_kv_kernel, tm=tm, tk=tk, prompt=prompt)
    return pl.pallas_call(
        kern,
        grid=(n, t // tm),
        in_specs=in_specs,
        out_specs=out_specs,
        out_shape=out_shape,
        scratch_shapes=scratch,
        compiler_params=_cparams(("arbitrary", "arbitrary")),
        name="kv_prompt" if prompt else "kv_sample",
    )(*args)


def _q_kernel(x_ref, g_ref, wq_ref, seg_ref, segt_ref, qg_ref, o_ref, *, transpose):
    xn = _rms(x_ref[...], g_ref[...])
    q = _head_rms(_dot(xn.astype(bf16), wq_ref[...]), seg_ref[...], segt_ref[...], qg_ref[...])
    if transpose:
        o_ref[...] = q.T.astype(bf16)
    else:
        o_ref[...] = q


def _q_proj(x, g, wq, qgain, layer, j, transpose):
    n, t, d = x.shape
    att = N_HEADS * HEAD_DIM
    tm = ROW_TILE if t % ROW_TILE == 0 else t
    seg, seg_t = _kv_constants()[:2]
    const = lambda shape: pl.BlockSpec(shape, lambda a, i: (0,) * len(shape))
    if transpose:
        out_spec = pl.BlockSpec((None, att, tm), lambda a, i: (a, 0, i))
        out_shape = jax.ShapeDtypeStruct((n, att, t), bf16)
    else:
        out_spec = pl.BlockSpec((None, tm, att), lambda a, i: (a, i, 0))
        out_shape = jax.ShapeDtypeStruct((n, t, att), f32)
    return pl.pallas_call(
        functools.partial(_q_kernel, transpose=transpose),
        grid=(n, t // tm),
        in_specs=[
            pl.BlockSpec((None, tm, d), lambda a, i: (a, i, 0)),
            pl.BlockSpec((None, 1, d), lambda a, i: (layer, 0, 0)),
            pl.BlockSpec((None, d, att), lambda a, i: (j, 0, 0)),
            const((att, LANES)), const((LANES, att)),
            pl.BlockSpec((None, 1, att), lambda a, i: (j, 0, 0)),
        ],
        out_specs=out_spec,
        out_shape=out_shape,
        compiler_params=_cparams(("arbitrary", "arbitrary")),
        name="q_proj",
    )(x, g, wq, seg, seg_t, qgain)


def _attn_prompt_kernel(qt_ref, ct_ref, kp_ref, vt_ref, o_ref, *, tq, tk):
    i = pl.program_id(2)
    nsub = tq // tk
    row = lax.broadcasted_iota(jnp.int32, (HEAD_DIM, tq), 0)
    ones_blk = (lax.broadcasted_iota(jnp.int32, (16, tk), 0) == 0).astype(bf16)
    kpos0 = lax.broadcasted_iota(jnp.int32, (tk, tq), 0)
    qpos = lax.broadcasted_iota(jnp.int32, (tk, tq), 1) + i * tq
    outs = []
    for e in range(2):
        hi, mid, lo = _split3(ct_ref[e:e + 1, :])
        aug = jnp.where(row == 0, hi.astype(f32),
                        jnp.where(row == 1, mid.astype(f32),
                                  jnp.where(row == 2, lo.astype(f32),
                                            jnp.where(row < N_AUG, 1.0, 0.0)))).astype(bf16)
        qe = qt_ref[e * HEAD_DIM:(e + 1) * HEAD_DIM, :]
        rhs = jnp.concatenate([qe, aug] if e == 0 else [aug, qe], axis=0)

        def step(j, carry, masked, e=e, rhs=rhs):
            m, acc = carry
            off = pl.multiple_of(j * tk, tk)
            kb = kp_ref[pl.ds(off, tk), e * LANES:(e + 1) * LANES]
            s = _dot(kb, rhs)
            if masked:
                s = jnp.where(kpos0 + off <= qpos, s, NEG_INF)
            m_new = jnp.maximum(m, jnp.max(s, axis=0, keepdims=True))
            p = jnp.exp2(s - m_new).astype(bf16)
            alpha = jnp.exp2(m - m_new)
            lhs = jnp.concatenate([vt_ref[j, e * HEAD_DIM:(e + 1) * HEAD_DIM, :], ones_blk], axis=0)
            acc = alpha * acc + _dot(lhs, p)
            return m_new, acc

        carry = (jnp.full((1, tq), NEG_INF, f32), jnp.zeros((HEAD_DIM + 16, tq), f32))
        n_full = i * nsub
        carry = lax.fori_loop(0, n_full, functools.partial(step, masked=False), carry)
        for r in range(nsub):
            carry = step(n_full + r, carry, masked=True)
        acc = carry[1]
        outs.append(acc[:HEAD_DIM] * pl.reciprocal(acc[HEAD_DIM:HEAD_DIM + 1], approx=False))
    o_ref[...] = jnp.concatenate(outs, axis=0).T.astype(bf16)


def _attn_prompt(qt, ct, kp, vt):
    n, att, t = qt.shape
    tq, tk = Q_TILE, K_TILE
    kern = functools.partial(_attn_prompt_kernel, tq=tq, tk=tk)
    return pl.pallas_call(
        kern,
        grid=(n, N_HEADS // 2, t // tq),
        in_specs=[
            pl.BlockSpec((None, 2 * HEAD_DIM, tq), lambda a, p, i: (a, p, i)),
            pl.BlockSpec((None, None, 2, tq), lambda a, p, i: (a, p, 0, i)),
            pl.BlockSpec((None, t, 2 * LANES), lambda a, p, i: (a, 0, p)),
            pl.BlockSpec((None, t // tk, 2 * HEAD_DIM, tk), lambda a, p, i: (a, 0, p, 0)),
        ],
        out_specs=pl.BlockSpec((None, tq, 2 * HEAD_DIM), lambda a, p, i: (a, i, p)),
        out_shape=jax.ShapeDtypeStruct((n, t, att), bf16),
        compiler_params=_cparams(("arbitrary", "arbitrary", "arbitrary")),
        name="attn_prompt",
    )(qt, ct, kp, vt)


def _attn_sample_kernel(pt_ref, q_ref, knew_ref, vnew_ref, lfnew_ref, ck_ref, cv_ref, clf_ref,
                        usuf_ref, uinc_ref, o_ref,
                        m_sc, l_sc, acc_sc, carry_sc, e_sc, kpage, vpage, *, n_pages, page, seq_len):
    b = pl.program_id(0)
    j = pl.program_id(1)
    hq = N_HEADS * 8
    lane = lax.broadcasted_iota(jnp.int32, (hq, page), 1)
    tq = lax.broadcasted_iota(jnp.int32, (hq, page), 0) % 8

    def rep8(x):
        return jnp.concatenate([jnp.broadcast_to(x[h:h + 1], (8, page)) for h in range(N_HEADS)], axis=0)

    def page_step(kref, vref, offs, valid):
        q8 = q_ref[...]
        s_list = []
        for h in range(N_HEADS):
            qh = q8[:, h * HEAD_DIM:(h + 1) * HEAD_DIM]
            kh = kref[pl.ds(h, page, stride=N_HEADS), :]
            s_list.append(lax.dot_general(qh, kh, (((1,), (1,)), ((), ())), preferred_element_type=f32))
        s = jnp.concatenate(s_list, axis=0) + offs
        if valid is not None:
            s = jnp.where(valid, s, NEG_INF)
        m_prev = m_sc[...]
        m_new = jnp.maximum(m_prev, jnp.max(s, axis=1, keepdims=True))
        p = jnp.exp2(s - m_new)
        alpha = jnp.exp2(m_prev - m_new)
        l_sc[...] = alpha * l_sc[...] + jnp.sum(p, axis=1, keepdims=True)
        pv = [_dot(p[h * 8:(h + 1) * 8], vref[pl.ds(h, page, stride=N_HEADS), :]) for h in range(N_HEADS)]
        acc_sc[...] = alpha * acc_sc[...] + jnp.concatenate(pv, axis=0)
        m_sc[...] = m_new

    @pl.when(jnp.logical_and(b == 0, j == 0))
    def _():
        kpage[...] = jnp.zeros(kpage.shape, f32)
        vpage[...] = jnp.zeros(vpage.shape, f32)

    @pl.when(j == 0)
    def _():
        e_t = _dot3(lfnew_ref[...], uinc_ref[...]) * LOG2E
        e_rep = rep8(e_t)
        e_col = jnp.sum(jnp.where(lane == tq, e_rep, 0.0), axis=1, keepdims=True)
        e_sc[...] = e_col
        carry_sc[...] = jnp.zeros(carry_sc.shape, f32)
        m_sc[...] = jnp.full(m_sc.shape, NEG_INF, f32)
        l_sc[...] = jnp.zeros(l_sc.shape, f32)
        acc_sc[...] = jnp.zeros(acc_sc.shape, f32)
        nrow = seq_len * N_HEADS
        kpage[0:nrow, :] = knew_ref[...]
        vpage[0:nrow, :] = vnew_ref[...]
        valid = jnp.logical_and(lane <= tq, lane < seq_len)
        page_step(kpage, vpage, e_col - e_rep, valid)

    dd = _dot3(clf_ref[...], usuf_ref[...])
    d_t = (dd[:, :page] + carry_sc[...]) * LOG2E
    carry_sc[...] = carry_sc[...] + dd[:, page:]
    page_step(ck_ref, cv_ref, rep8(d_t) + e_sc[...], None)

    @pl.when(j == n_pages - 1)
    def _():
        o = acc_sc[...] * pl.reciprocal(l_sc[...], approx=False)
        for h in range(N_HEADS):
            o_ref[:, h * HEAD_DIM:(h + 1) * HEAD_DIM] = o[h * 8:(h + 1) * 8, :]


def _attn_sample(q, k_new, v_new, lf_new_t, cache_k2, cache_v2, cache_lf_t, page_table, seq_len):
    nb = q.shape[0]
    att = N_HEADS * HEAD_DIM
    n_pages = page_table.shape[1]
    page = cache_lf_t.shape[2]
    rows = cache_k2.shape[1]
    assert page == LANES and rows == page * N_HEADS
    usuf = np.concatenate([np.triu(np.ones((page, page), np.float32), 1).T, np.ones((page, page), np.float32)], axis=1)
    uinc = np.triu(np.ones((page, page), np.float32))
    pt = page_table.reshape(-1).astype(jnp.int32)
    seq = lambda shape: pl.BlockSpec((None,) + shape, lambda b, j, pt: (b, 0, 0))
    pool = lambda shape: pl.BlockSpec((None,) + shape,
                                      lambda b, j, pt: (pt[b * n_pages + n_pages - 1 - j], 0, 0))
    const = lambda shape: pl.BlockSpec(shape, lambda b, j, pt: (0, 0))
    hq = N_HEADS * 8
    kern = functools.partial(_attn_sample_kernel, n_pages=n_pages, page=page, seq_len=seq_len)
    return pl.pallas_call(
        kern,
        grid_spec=pltpu.PrefetchScalarGridSpec(
            num_scalar_prefetch=1,
            grid=(nb, n_pages),
            in_specs=[seq((8, att)), seq((seq_len * N_HEADS, HEAD_DIM)), seq((seq_len * N_HEADS, HEAD_DIM)),
                      seq((N_HEADS, page)), pool((rows, HEAD_DIM)), pool((rows, HEAD_DIM)), pool((N_HEADS, page)),
                      const((page, 2 * page)), const((page, page))],
            out_specs=seq((8, att)),
            scratch_shapes=[pltpu.VMEM((hq, 1), f32), pltpu.VMEM((hq, 1), f32), pltpu.VMEM((hq, HEAD_DIM), f32),
                            pltpu.VMEM((N_HEADS, page), f32), pltpu.VMEM((hq, 1), f32),
                            pltpu.VMEM((rows, HEAD_DIM), f32), pltpu.VMEM((rows, HEAD_DIM), f32)],
        ),
        out_shape=jax.ShapeDtypeStruct((nb, 8, att), f32),
        compiler_params=_cparams(("arbitrary", "arbitrary")),
        name="attn_sample",
    )(pt, q, k_new, v_new, lf_new_t, cache_k2, cache_v2, cache_lf_t, jnp.asarray(usuf, bf16), jnp.asarray(uinc, bf16))


def kernel(x_prompt, x_sample, cache_k, cache_v, cache_logf, state_conv, page_table, meta_tokens, norm_mix, norm_mlp, conv_w_in, conv_w, conv_w_out, norm_kv, w_kv, k_norm, w_f, b_f, w_q, q_norm, w_o, mlp_w1, mlp_w2):
    nb, seq, d = x_prompt.shape
    db, dseq, _ = x_sample.shape
    att = N_HEADS * HEAD_DIM
    n_a = conv_w_in.shape[0]
    depth = norm_mix.shape[0]
    t_real = seq + N_META
    pad_to = ROW_TILE * Q_TILE // np.gcd(ROW_TILE, Q_TILE)
    tp = -(-t_real // pad_to) * pad_to

    g_mix = norm_mix.reshape(depth, 1, d)
    g_mlp = norm_mlp.reshape(depth, 1, d)
    w_in = conv_w_in.astype(bf16)
    w_out = conv_w_out.astype(bf16)
    w1 = mlp_w1.astype(bf16)
    w2 = mlp_w2.astype(bf16)
    wk = w_kv[:, :att].astype(bf16)
    wv = w_kv[:, att:].astype(bf16)
    wf = jnp.pad(w_f, ((0, 0), (0, LANES - N_HEADS))).astype(bf16)
    bfp = jnp.pad(b_f, (0, LANES - N_HEADS)).reshape(1, LANES)
    g_kv = norm_kv.reshape(1, d)
    kgain = jnp.tile(k_norm, N_HEADS).reshape(1, att)
    wq = w_q.astype(bf16)
    qgain = (jnp.tile(q_norm, (1, N_HEADS)) * (HEAD_DIM ** -0.5 * LOG2E)).reshape(-1, 1, att)
    wo = w_o.astype(bf16)

    meta = jnp.broadcast_to(meta_tokens[None], (nb, N_META, d))
    xp = jnp.concatenate([meta, x_prompt, jnp.zeros((nb, tp - t_real, d), f32)], axis=1)
    conv_rows = []
    for layer in range(n_a):
        xp, st = _conv_prompt(xp, g_mix, w_in, conv_w, w_out, layer, t_real)
        conv_rows.append(st)
        xp = _mlp(xp.reshape(nb * tp, d), g_mlp, w1, w2, layer).reshape(nb, tp, d)
    k_p, v_p, lf_p, c_p, kp, vt = _kv_side(xp, g_kv, wk, wv, wf, bfp, kgain, prompt=True)
    ct = jnp.swapaxes(c_p[:, :, :N_HEADS], 1, 2).reshape(nb, N_HEADS // 2, 2, tp)
    for j in range(depth - n_a):
        layer = n_a + j
        qt = _q_proj(xp, g_mix, wq, qgain, layer, j, transpose=True)
        o = _attn_prompt(qt, ct, kp, vt)
        xp = _mlp(xp.reshape(nb * tp, d), g_mlp, w1, w2, layer,
                  proj=(o.reshape(nb * tp, att), wo, j)).reshape(nb, tp, d)
    y_prompt = xp[:, N_META:t_real]
    k_prompt = k_p[:, :t_real].reshape(nb, t_real, N_HEADS, HEAD_DIM)
    v_prompt = v_p[:, :t_real].reshape(nb, t_real, N_HEADS, HEAD_DIM)
    logf_prompt = lf_p[:, :t_real, :N_HEADS]
    conv_prompt = jnp.stack(conv_rows)

    m_s = db * dseq
    xs = x_sample.reshape(m_s, d)
    conv_rows = []
    for layer in range(n_a):
        xs, st = _conv_sample(xs, state_conv[layer], g_mix, w_in, conv_w, w_out, layer, dseq)
        conv_rows.append(st)
        xs = _mlp(xs, g_mlp, w1, w2, layer)
    k_s, v_s, lf_s = _kv_side(xs.reshape(1, m_s, d), g_kv, wk, wv, wf, bfp, kgain, prompt=False)
    n_pool, page = cache_k.shape[0], cache_k.shape[1]
    cache_k2 = cache_k.reshape(n_pool, page * N_HEADS, HEAD_DIM)
    cache_v2 = cache_v.reshape(n_pool, page * N_HEADS, HEAD_DIM)
    cache_lf_t = jnp.swapaxes(cache_logf, 1, 2)
    k_new = k_s.reshape(db, dseq * N_HEADS, HEAD_DIM)
    v_new = v_s.reshape(db, dseq * N_HEADS, HEAD_DIM)
    lf_new = lf_s.reshape(db, dseq, LANES)[:, :, :N_HEADS]
    lf_new_t = jnp.pad(jnp.swapaxes(lf_new, 1, 2), ((0, 0), (0, 0), (0, page - dseq)))
    for j in range(depth - n_a):
        layer = n_a + j
        q = _q_proj(xs.reshape(1, m_s, d), g_mix, wq, qgain, layer, j, transpose=False)
        q8 = jnp.pad(q.reshape(db, dseq, att), ((0, 0), (0, 8 - dseq), (0, 0)))
        o8 = _attn_sample(q8, k_new, v_new, lf_new_t, cache_k2, cache_v2, cache_lf_t, page_table, dseq)
        o = o8[:, :dseq].reshape(m_s, att).astype(bf16)
        xs = _mlp(xs, g_mlp, w1, w2, layer, proj=(o, wo, j))
    y_sample = xs.reshape(db, dseq, d)
    k_sample = k_s.reshape(db, dseq, N_HEADS, HEAD_DIM)
    v_sample = v_s.reshape(db, dseq, N_HEADS, HEAD_DIM)
    logf_sample = lf_s.reshape(db, dseq, LANES)[:, :, :N_HEADS]
    conv_sample = jnp.stack(conv_rows)

    return (y_prompt, y_sample, k_prompt, v_prompt, logf_prompt, conv_prompt,
            k_sample, v_sample, logf_sample, conv_sample)
```

```python
import functools

import numpy as np
import jax
import jax.numpy as jnp
from jax import lax
from jax.experimental import pallas as pl
from jax.experimental.pallas import tpu as pltpu

N_HEADS = 16
HEAD_DIM = 64
N_META = 16
CONV_WIDTH = 3
RMS_EPS = 1e-6
NEG_INF = -1e30
LOG2E = 1.4426950408889634

LANES = 128
VMEM_LIMIT = 56 * 1024 * 1024
ROW_TILE = 384
ATT_TILE = 256
KV_ROW_TILE = ATT_TILE
HEAD_GROUP = 4
SAMPLE_PAGES = 4
N_AUG = 6

f32 = jnp.float32
bf16 = jnp.bfloat16


def _cparams(sem):
    return pltpu.CompilerParams(dimension_semantics=sem, vmem_limit_bytes=VMEM_LIMIT)


def _dot(a, b):
    return jnp.dot(a, b, preferred_element_type=f32)


def _rms(x, g):
    ms = jnp.mean(x * x, axis=-1, keepdims=True)
    return x * lax.rsqrt(ms + RMS_EPS) * g


def _split3(x):
    hi = x.astype(bf16)
    r1 = x - hi.astype(f32)
    mid = r1.astype(bf16)
    lo = (r1 - mid.astype(f32)).astype(bf16)
    return hi, mid, lo


def _dot3(x, w):
    hi, mid, lo = _split3(x)
    return _dot(hi, w) + _dot(mid, w) + _dot(lo, w)


def _dot3_left(w, x):
    hi, mid, lo = _split3(x)
    return _dot(w, hi) + _dot(w, mid) + _dot(w, lo)


def _head_rms(y, seg, seg_t, gain):
    ssq = _dot((y * y).astype(bf16), seg)
    inv = lax.rsqrt(ssq * (1.0 / HEAD_DIM) + RMS_EPS)
    hi = inv.astype(bf16)
    lo = (inv - hi.astype(f32)).astype(bf16)
    inv_full = _dot(hi, seg_t) + _dot(lo, seg_t)
    return y * inv_full * gain


def _conv_kernel(*refs, tm, mode, st_tile, st_row, seq_len):
    if mode == "prompt":
        x_ref, g_ref, win_ref, cw_ref, wout_ref, o_ref, st_ref, ubuf = refs
    else:
        x_ref, s1_ref, s2_ref, g_ref, win_ref, cw_ref, wout_ref, o_ref, st_ref = refs
    x = x_ref[...]
    d = x.shape[1]
    xn = _rms(x, g_ref[...])
    bch = _dot(xn.astype(bf16), win_ref[...])
    b = bch[:, :d]
    u = bch[:, d:2 * d] * bch[:, 2 * d:]
    if mode == "prompt":
        i = pl.program_id(1)

        @pl.when(i == 0)
        def _():
            ubuf[0:8, :] = jnp.zeros((8, d), f32)

        @pl.when(i > 0)
        def _():
            ubuf[0:8, :] = ubuf[tm:tm + 8, :]

        ubuf[8:tm + 8, :] = u
        prev1 = ubuf[7:tm + 7, :]
        prev2 = ubuf[6:tm + 6, :]

        @pl.when(i == st_tile)
        def _():
            st_ref[...] = u[st_row:st_row + 8, :]
    else:
        t = lax.broadcasted_iota(jnp.int32, (tm, d), 0) % seq_len
        prev1 = jnp.where(t >= 1, pltpu.roll(u, 1, 0), s1_ref[...])
        prev2 = jnp.where(t >= 2, pltpu.roll(u, 2, 0), s2_ref[...])
        st_ref[...] = u
    cw = cw_ref[...]
    conv = cw[2:3] * u + cw[0:1] * prev2 + cw[1:2] * prev1
    y = _dot((b * conv).astype(bf16), wout_ref[...])
    o_ref[...] = x + y


def _conv_prompt(x, g, w_in, cw, w_out, layer, t_real):
    n, tp, d = x.shape
    tm = ROW_TILE
    assert tp % tm == 0 and (t_real - 2) % 8 == 6
    st_tile, st_off = divmod(t_real - 2, tm)
    st_row = st_off // 8 * 8
    kern = functools.partial(_conv_kernel, tm=tm, mode="prompt", st_tile=st_tile, st_row=st_row, seq_len=0)
    out, st = pl.pallas_call(
        kern,
        grid=(n, tp // tm),
        in_specs=[
            pl.BlockSpec((None, tm, d), lambda a, i: (a, i, 0)),
            pl.BlockSpec((None, 1, d), lambda a, i: (layer, 0, 0)),
            pl.BlockSpec((None, d, 3 * d), lambda a, i: (layer, 0, 0)),
            pl.BlockSpec((None, CONV_WIDTH, d), lambda a, i: (layer, 0, 0)),
            pl.BlockSpec((None, d, d), lambda a, i: (layer, 0, 0)),
        ],
        out_specs=[
            pl.BlockSpec((None, tm, d), lambda a, i: (a, i, 0)),
            pl.BlockSpec((None, 8, d), lambda a, i: (a, 0, 0)),
        ],
        out_shape=[jax.ShapeDtypeStruct((n, tp, d), f32), jax.ShapeDtypeStruct((n, 8, d), f32)],
        scratch_shapes=[pltpu.VMEM((tm + 8, d), f32)],
        compiler_params=_cparams(("arbitrary", "arbitrary")),
        name="conv_prompt",
    )(x, g, w_in, cw, w_out)
    return out, st[:, 6:8]


def _conv_sample(x, state, g, w_in, cw, w_out, layer, seq_len):
    m, d = x.shape
    nb = m // seq_len
    z = jnp.zeros((nb, seq_len - 1, d), f32)
    s1 = jnp.concatenate([state[:, 1:2], z], axis=1).reshape(m, d)
    s2 = jnp.concatenate([state[:, 0:1], state[:, 1:2], z[:, 1:]], axis=1).reshape(m, d)
    kern = functools.partial(_conv_kernel, tm=m, mode="sample", st_tile=0, st_row=0, seq_len=seq_len)
    full = lambda i: (0, 0)
    out, u = pl.pallas_call(
        kern,
        grid=(1,),
        in_specs=[
            pl.BlockSpec((m, d), full),
            pl.BlockSpec((m, d), full),
            pl.BlockSpec((m, d), full),
            pl.BlockSpec((None, 1, d), lambda i: (layer, 0, 0)),
            pl.BlockSpec((None, d, 3 * d), lambda i: (layer, 0, 0)),
            pl.BlockSpec((None, CONV_WIDTH, d), lambda i: (layer, 0, 0)),
            pl.BlockSpec((None, d, d), lambda i: (layer, 0, 0)),
        ],
        out_specs=[pl.BlockSpec((m, d), full), pl.BlockSpec((m, d), full)],
        out_shape=[jax.ShapeDtypeStruct((m, d), f32), jax.ShapeDtypeStruct((m, d), f32)],
        compiler_params=_cparams(("arbitrary",)),
        name="conv_sample",
    )(x, s1, s2, g, w_in, cw, w_out)
    return out, u.reshape(nb, seq_len, d)[:, seq_len - 2:]


def _mlp_kernel(*refs, has_proj, n_chunks):
    if has_proj:
        x_ref, a_ref, wa_ref, g_ref, w1_ref, w2_ref, o_ref = refs
    else:
        x_ref, g_ref, w1_ref, w2_ref, o_ref = refs
    x = x_ref[...]
    if has_proj:
        x = x + _dot(a_ref[...], wa_ref[...])
    xn = _rms(x, g_ref[...]).astype(bf16)
    ck = w1_ref.shape[1] // n_chunks
    y = x
    for c in range(n_chunks):
        h = _dot(xn, w1_ref[:, c * ck:(c + 1) * ck])
        h = jnp.square(jnp.maximum(h, 0.0)).astype(bf16)
        y = y + _dot(h, w2_ref[c * ck:(c + 1) * ck, :])
    o_ref[...] = y


def _mlp(x, g, w1, w2, layer, proj=None):
    m, d = x.shape
    dff = w1.shape[2]
    tm = ROW_TILE if m % ROW_TILE == 0 else m
    row = lambda i: (i, 0)
    in_specs = [pl.BlockSpec((tm, d), row)]
    args = [x]
    if proj is not None:
        a, wa, pl_idx = proj
        in_specs += [pl.BlockSpec((tm, a.shape[1]), row),
                     pl.BlockSpec((None, a.shape[1], d), lambda i: (pl_idx, 0, 0))]
        args += [a, wa]
    in_specs += [
        pl.BlockSpec((None, 1, d), lambda i: (layer, 0, 0)),
        pl.BlockSpec((None, d, dff), lambda i: (layer, 0, 0)),
        pl.BlockSpec((None, dff, d), lambda i: (layer, 0, 0)),
    ]
    args += [g, w1, w2]
    kern = functools.partial(_mlp_kernel, has_proj=proj is not None, n_chunks=4)
    return pl.pallas_call(
        kern,
        grid=(m // tm,),
        in_specs=in_specs,
        out_specs=pl.BlockSpec((tm, d), row),
        out_shape=jax.ShapeDtypeStruct((m, d), f32),
        compiler_params=_cparams(("arbitrary",)),
        name="mlp",
    )(*args)


def _aug_offset(h):
    return LANES * h + (1 - h % 2) * HEAD_DIM


def _kv_constants():
    att = N_HEADS * HEAD_DIM
    seg = np.zeros((att, LANES), np.float32)
    seg[np.arange(att), np.arange(att) // HEAD_DIM] = 1.0
    pneg = np.zeros((3, LANES, N_HEADS * LANES), np.float32)
    ones_row = np.zeros((1, N_HEADS * LANES), np.float32)
    kmask = np.zeros((1, N_HEADS * LANES), np.float32)
    for h in range(N_HEADS):
        a = _aug_offset(h)
        ones_row[0, a:a + 3] = 1.0
        for part in range(3):
            pneg[part, h, a + 3 + part] = -1.0
        k0 = LANES * h + (h % 2) * HEAD_DIM
        kmask[0, k0:k0 + HEAD_DIM] = 1.0
    return (jnp.asarray(seg, bf16), jnp.asarray(seg.T, bf16), jnp.asarray(pneg, bf16),
            jnp.asarray(ones_row), jnp.asarray(kmask))


def _kv_kernel(*refs, tm, tk, prompt):
    if prompt:
        (x_ref, g_ref, wk_ref, wv_ref, wf_ref, bf_ref, seg_ref, segt_ref, kg_ref,
         ltri_ref, pneg_ref, ones_ref, kmask_ref,
         k_ref, v_ref, lf_ref, c_ref, kp_ref, vt_ref, carry) = refs
    else:
        (x_ref, g_ref, wk_ref, wv_ref, wf_ref, bf_ref, seg_ref, segt_ref, kg_ref,
         k_ref, v_ref, lf_ref) = refs
    xn = _rms(x_ref[...], g_ref[...])
    xb = xn.astype(bf16)
    k = _head_rms(_dot(xb, wk_ref[...]), seg_ref[...], segt_ref[...], kg_ref[...])
    v = _dot(xb, wv_ref[...])
    z = _dot(xb, wf_ref[...]) + bf_ref[...]
    lane = lax.broadcasted_iota(jnp.int32, z.shape, 1)
    logf = jnp.where(lane < N_HEADS, jax.nn.log_sigmoid(z), 0.0)
    k_ref[...] = k
    v_ref[...] = v
    lf_ref[...] = logf
    if not prompt:
        return

    @pl.when(pl.program_id(1) == 0)
    def _():
        carry[...] = jnp.zeros(carry.shape, f32)

    c = _dot3_left(ltri_ref[...], logf) + carry[0:1, :]
    carry[...] = jnp.broadcast_to(c[tm - 1:tm, :], carry.shape)
    c2 = c * LOG2E
    c_ref[...] = c2
    hi, mid, lo = _split3(c2)
    aug = _dot(hi, pneg_ref[0]) + _dot(mid, pneg_ref[1]) + _dot(lo, pneg_ref[2]) + ones_ref[...]
    pairs = []
    for p in range(N_HEADS // 2):
        blk = k[:, p * LANES:(p + 1) * LANES]
        pairs += [blk, blk]
    k_rep = jnp.concatenate(pairs, axis=1)
    kp_ref[...] = jnp.where(kmask_ref[...] > 0.5, k_rep, aug).astype(bf16)
    vt = v.T.astype(bf16)
    for j in range(tm // tk):
        vt_ref[j] = vt[:, j * tk:(j + 1) * tk]


def _kv_side(x, g, wk, wv, wf, bfp, kgain, prompt):
    n, t, d = x.shape
    att = N_HEADS * HEAD_DIM
    tm = KV_ROW_TILE if prompt else t
    tk = ATT_TILE
    assert t % tm == 0
    seg, seg_t, pneg, ones_row, kmask = _kv_constants()
    const = lambda shape: pl.BlockSpec(shape, lambda a, i: (0,) * len(shape))
    row = lambda w: pl.BlockSpec((None, tm, w), lambda a, i: (a, i, 0))
    in_specs = [row(d), const((1, d)), const((d, att)), const((d, att)), const((d, LANES)),
                const((1, LANES)), const((att, LANES)), const((LANES, att)), const((1, att))]
    args = [x, g, wk, wv, wf, bfp, seg, seg_t, kgain]
    out_specs = [row(att), row(att), row(LANES)]
    out_shape = [jax.ShapeDtypeStruct((n, t, att), f32), jax.ShapeDtypeStruct((n, t, att), f32),
                 jax.ShapeDtypeStruct((n, t, LANES), f32)]
    scratch = []
    if prompt:
        ltri = jnp.asarray(np.tril(np.ones((tm, tm), np.float32)), bf16)
        in_specs += [const((tm, tm)), const((3, LANES, N_HEADS * LANES)),
                     const((1, N_HEADS * LANES)), const((1, N_HEADS * LANES))]
        args += [ltri, pneg, ones_row, kmask]
        out_specs += [row(LANES), row(N_HEADS * LANES),
                      pl.BlockSpec((None, tm // tk, att, tk), lambda a, i: (a, i, 0, 0))]
        out_shape += [jax.ShapeDtypeStruct((n, t, LANES), f32),
                      jax.ShapeDtypeStruct((n, t, N_HEADS * LANES), bf16),
                      jax.ShapeDtypeStruct((n, t // tk, att, tk), bf16)]
        scratch = [pltpu.VMEM((8, LANES), f32)]
    kern = functools.partial(_kv_kernel, tm=tm, tk=tk, prompt=prompt)
    return pl.pallas_call(
        kern,
        grid=(n, t // tm),
        in_specs=in_specs,
        out_specs=out_specs,
        out_shape=out_shape,
        scratch_shapes=scratch,
        compiler_params=_cparams(("arbitrary", "arbitrary")),
        name="kv_prompt" if prompt else "kv_sample",
    )(*args)


def _q_kernel(x_ref, g_ref, wq_ref, seg_ref, segt_ref, qg_ref, o_ref, *, transpose):
    xn = _rms(x_ref[...], g_ref[...])
    q = _head_rms(_dot(xn.astype(bf16), wq_ref[...]), seg_ref[...], segt_ref[...], qg_ref[...])
    if transpose:
        o_ref[...] = q.T.astype(bf16)
    else:
        o_ref[...] = q


def _q_proj(x, g, wq, qgain, layer, j, transpose):
    n, t, d = x.shape
    att = N_HEADS * HEAD_DIM
    tm = ROW_TILE if t % ROW_TILE == 0 else t
    seg, seg_t = _kv_constants()[:2]
    const = lambda shape: pl.BlockSpec(shape, lambda a, i: (0,) * len(shape))
    if transpose:
        out_spec = pl.BlockSpec((None, att, tm), lambda a, i: (a, 0, i))
        out_shape = jax.ShapeDtypeStruct((n, att, t), bf16)
    else:
        out_spec = pl.BlockSpec((None, tm, att), lambda a, i: (a, i, 0))
        out_shape = jax.ShapeDtypeStruct((n, t, att), f32)
    return pl.pallas_call(
        functools.partial(_q_kernel, transpose=transpose),
        grid=(n, t // tm),
        in_specs=[
            pl.BlockSpec((None, tm, d), lambda a, i: (a, i, 0)),
            pl.BlockSpec((None, 1, d), lambda a, i: (layer, 0, 0)),
            pl.BlockSpec((None, d, att), lambda a, i: (j, 0, 0)),
            const((att, LANES)), const((LANES, att)),
            pl.BlockSpec((None, 1, att), lambda a, i: (j, 0, 0)),
        ],
        out_specs=out_spec,
        out_shape=out_shape,
        compiler_params=_cparams(("arbitrary", "arbitrary")),
        name="q_proj",
    )(x, g, wq, seg, seg_t, qgain)


def _attn_prompt_kernel(qt_ref, ct_ref, kp_ref, vt_ref, o_ref, rhs_sc, s_sc, m_sc, acc_sc, *, t, hg):
    i = pl.program_id(2)
    row = lax.broadcasted_iota(jnp.int32, (HEAD_DIM, t), 0)
    for e in range(hg):
        hi, mid, lo = _split3(ct_ref[e:e + 1, :])
        aug = jnp.where(row == 0, hi.astype(f32),
                        jnp.where(row == 1, mid.astype(f32),
                                  jnp.where(row == 2, lo.astype(f32),
                                            jnp.where(row < N_AUG, 1.0, 0.0)))).astype(bf16)
        qe = qt_ref[e * HEAD_DIM:(e + 1) * HEAD_DIM, :]
        rhs_sc[e] = jnp.concatenate([qe, aug] if e % 2 == 0 else [aug, qe], axis=0)
        m_sc[e] = jnp.full((1, t), NEG_INF, f32)
        acc_sc[e] = jnp.zeros((HEAD_DIM + 16, t), f32)
    ones_blk = (lax.broadcasted_iota(jnp.int32, (16, t), 0) == 0).astype(bf16)
    causal = lax.broadcasted_iota(jnp.int32, (t, t), 0) <= lax.broadcasted_iota(jnp.int32, (t, t), 1)

    def scores(j, slot):
        off = pl.multiple_of(j * t, t)
        for e in range(hg):
            kb = kp_ref[pl.ds(off, t), e * LANES:(e + 1) * LANES]
            s_sc[slot, e] = _dot(kb, rhs_sc[e])

    def update(j, slot, masked):
        for e in range(hg):
            s = s_sc[slot, e]
            if masked:
                s = jnp.where(causal, s, NEG_INF)
            m = m_sc[e]
            m_new = jnp.maximum(m, jnp.max(s, axis=0, keepdims=True))
            p = jnp.exp2(s - m_new).astype(bf16)
            alpha = jnp.exp2(m - m_new)
            lhs = jnp.concatenate([vt_ref[j, e * HEAD_DIM:(e + 1) * HEAD_DIM, :], ones_blk], axis=0)
            acc_sc[e] = alpha * acc_sc[e] + _dot(lhs, p)
            m_sc[e] = m_new

    scores(0, 0)

    def pair(u, c):
        j = 2 * u
        scores(j + 1, 1)
        update(j, 0, False)
        scores(j + 2, 0)
        update(j + 1, 1, False)
        return c

    lax.fori_loop(0, lax.shift_right_logical(i, 1), pair, 0)

    @pl.when(i % 2 == 0)
    def _():
        update(i, 0, True)

    @pl.when(i % 2 == 1)
    def _():
        scores(i, 1)
        update(i - 1, 0, False)
        update(i, 1, True)

    outs = []
    for e in range(hg):
        acc = acc_sc[e]
        outs.append(acc[:HEAD_DIM] * pl.reciprocal(acc[HEAD_DIM:HEAD_DIM + 1], approx=False))
    o_ref[...] = jnp.concatenate(outs, axis=0).T.astype(bf16)


def _attn_prompt(qt, ct, kp, vt):
    n, att, tlen = qt.shape
    t, hg = ATT_TILE, HEAD_GROUP
    kern = functools.partial(_attn_prompt_kernel, t=t, hg=hg)
    return pl.pallas_call(
        kern,
        grid=(n, N_HEADS // hg, tlen // t),
        in_specs=[
            pl.BlockSpec((None, hg * HEAD_DIM, t), lambda a, g, i: (a, g, i)),
            pl.BlockSpec((None, None, hg, t), lambda a, g, i: (a, g, 0, i)),
            pl.BlockSpec((None, tlen, hg * LANES), lambda a, g, i: (a, 0, g)),
            pl.BlockSpec((None, tlen // t, hg * HEAD_DIM, t), lambda a, g, i: (a, 0, g, 0)),
        ],
        out_specs=pl.BlockSpec((None, t, hg * HEAD_DIM), lambda a, g, i: (a, i, g)),
        out_shape=jax.ShapeDtypeStruct((n, tlen, att), bf16),
        scratch_shapes=[pltpu.VMEM((hg, LANES, t), bf16), pltpu.VMEM((2, hg, t, t), f32),
                        pltpu.VMEM((hg, 1, t), f32), pltpu.VMEM((hg, HEAD_DIM + 16, t), f32)],
        compiler_params=_cparams(("arbitrary", "arbitrary", "arbitrary")),
        name="attn_prompt",
    )(qt, ct, kp, vt)


def _dot_nt(a, b):
    return lax.dot_general(a, b, (((1,), (1,)), ((), ())), preferred_element_type=f32)


def _attn_sample_kernel(*refs, n_steps, pps, page, seq_len):
    pt_ref, q_ref, knew_ref, vnew_ref, lfnew_ref = refs[:5]
    ck_refs = refs[5:5 + pps]
    cv_refs = refs[5 + pps:5 + 2 * pps]
    clf_refs = refs[5 + 2 * pps:5 + 3 * pps]
    usuf_ref, uinc_ref, o_ref, m_sc, l_sc, acc_sc, carry_sc, e_sc, kpage, vpage = refs[5 + 3 * pps:]
    b = pl.program_id(0)
    j = pl.program_id(1)
    hq = N_HEADS * 8
    att = N_HEADS * HEAD_DIM

    def rep8(x):
        return jnp.concatenate([jnp.broadcast_to(x[h:h + 1], (8, page)) for h in range(N_HEADS)], axis=0)

    def heads(a):
        return [a[:, h * HEAD_DIM:(h + 1) * HEAD_DIM] for h in range(N_HEADS)]

    def softmax_update(s, pv_fn):
        m_prev = m_sc[...]
        m_new = jnp.maximum(m_prev, jnp.max(s, axis=1, keepdims=True))
        p = jnp.exp2(s - m_new)
        alpha = jnp.exp2(m_prev - m_new)
        l_sc[...] = alpha * l_sc[...] + jnp.sum(p, axis=1, keepdims=True)
        acc_sc[...] = alpha * acc_sc[...] + pv_fn(p)
        m_sc[...] = m_new

    @pl.when(jnp.logical_and(b == 0, j == 0))
    def _():
        kpage[...] = jnp.zeros(kpage.shape, f32)
        vpage[...] = jnp.zeros(vpage.shape, f32)

    @pl.when(j == 0)
    def _():
        lane = lax.broadcasted_iota(jnp.int32, (hq, page), 1)
        tq = lax.broadcasted_iota(jnp.int32, (hq, page), 0) % 8
        e_t = _dot3(lfnew_ref[...], uinc_ref[...]) * LOG2E
        e_rep = rep8(e_t)
        e_col = jnp.sum(jnp.where(lane == tq, e_rep, 0.0), axis=1, keepdims=True)
        e_sc[...] = e_col
        carry_sc[...] = jnp.zeros(carry_sc.shape, f32)
        m_sc[...] = jnp.full(m_sc.shape, NEG_INF, f32)
        l_sc[...] = jnp.zeros(l_sc.shape, f32)
        acc_sc[...] = jnp.zeros(acc_sc.shape, f32)
        kpage[0:seq_len, :] = knew_ref[...]
        vpage[0:seq_len, :] = vnew_ref[...]
        kh, vh = heads(kpage[...]), heads(vpage[...])
        s = jnp.concatenate([_dot_nt(qh, k) for qh, k in zip(heads(q_ref[...]), kh)], axis=0)
        valid = jnp.logical_and(lane <= tq, lane < seq_len)
        s = jnp.where(valid, s + (e_col - e_rep), NEG_INF)
        softmax_update(s, lambda p: jnp.concatenate(
            [_dot(p[h * 8:(h + 1) * 8], vh[h]) for h in range(N_HEADS)], axis=0))

    qh = heads(q_ref[...])
    s_pages = []
    for r in range(pps):
        dd = _dot3(clf_refs[r][...], usuf_ref[...])
        d_t = (dd[:, :page] + carry_sc[...]) * LOG2E
        carry_sc[...] = carry_sc[...] + dd[:, page:]
        s_r = jnp.concatenate([_dot(qh[h], ck_refs[r][h]) for h in range(N_HEADS)], axis=0)
        s_pages.append(s_r + rep8(d_t) + e_sc[...])
    s = jnp.concatenate(s_pages, axis=1)

    def pv_past(p):
        out = []
        for h in range(N_HEADS):
            acc = None
            for r in range(pps):
                term = _dot_nt(p[h * 8:(h + 1) * 8, r * page:(r + 1) * page], cv_refs[r][h])
                acc = term if acc is None else acc + term
            out.append(acc)
        return jnp.concatenate(out, axis=0)

    softmax_update(s, pv_past)

    @pl.when(j == n_steps - 1)
    def _():
        o = acc_sc[...] * pl.reciprocal(l_sc[...], approx=False)
        for h in range(N_HEADS):
            o_ref[:, h * HEAD_DIM:(h + 1) * HEAD_DIM] = o[h * 8:(h + 1) * 8, :]


def _attn_sample(q, k_new, v_new, lf_new_t, cache_kt, cache_vt, cache_lf_t, page_table, seq_len):
    nb = q.shape[0]
    att = N_HEADS * HEAD_DIM
    n_pages = page_table.shape[1]
    page = cache_lf_t.shape[2]
    pps = SAMPLE_PAGES if n_pages % SAMPLE_PAGES == 0 else 1
    n_steps = n_pages // pps
    assert page == LANES
    usuf = np.concatenate([np.triu(np.ones((page, page), np.float32), 1).T, np.ones((page, page), np.float32)], axis=1)
    uinc = np.triu(np.ones((page, page), np.float32))
    pt = page_table.reshape(-1).astype(jnp.int32)
    seq = lambda shape: pl.BlockSpec((None,) + shape, lambda b, j, pt: (b,) + (0,) * len(shape))

    def pool(shape, r):
        return pl.BlockSpec((None,) + shape,
                            lambda b, j, pt: (pt[b * n_pages + n_pages - 1 - (j * pps + r)],) + (0,) * len(shape))

    const = lambda shape: pl.BlockSpec(shape, lambda b, j, pt: (0, 0))
    hq = N_HEADS * 8
    kern = functools.partial(_attn_sample_kernel, n_steps=n_steps, pps=pps, page=page, seq_len=seq_len)
    in_specs = [seq((8, att)), seq((seq_len, att)), seq((seq_len, att)), seq((N_HEADS, page))]
    in_specs += [pool((N_HEADS, HEAD_DIM, page), r) for r in range(pps)]
    in_specs += [pool((N_HEADS, HEAD_DIM, page), r) for r in range(pps)]
    in_specs += [pool((N_HEADS, page), r) for r in range(pps)]
    in_specs += [const((page, 2 * page)), const((page, page))]
    return pl.pallas_call(
        kern,
        grid_spec=pltpu.PrefetchScalarGridSpec(
            num_scalar_prefetch=1,
            grid=(nb, n_steps),
            in_specs=in_specs,
            out_specs=seq((8, att)),
            scratch_shapes=[pltpu.VMEM((hq, 1), f32), pltpu.VMEM((hq, 1), f32), pltpu.VMEM((hq, HEAD_DIM), f32),
                            pltpu.VMEM((N_HEADS, page), f32), pltpu.VMEM((hq, 1), f32),
                            pltpu.VMEM((page, att), f32), pltpu.VMEM((page, att), f32)],
        ),
        out_shape=jax.ShapeDtypeStruct((nb, 8, att), f32),
        compiler_params=_cparams(("arbitrary", "arbitrary")),
        name="attn_sample",
    )(pt, q, k_new, v_new, lf_new_t, *([cache_kt] * pps), *([cache_vt] * pps), *([cache_lf_t] * pps),
      jnp.asarray(usuf, bf16), jnp.asarray(uinc, bf16))


def kernel(x_prompt, x_sample, cache_k, cache_v, cache_logf, state_conv, page_table, meta_tokens, norm_mix, norm_mlp, conv_w_in, conv_w, conv_w_out, norm_kv, w_kv, k_norm, w_f, b_f, w_q, q_norm, w_o, mlp_w1, mlp_w2):
    nb, seq, d = x_prompt.shape
    db, dseq, _ = x_sample.shape
    att = N_HEADS * HEAD_DIM
    n_a = conv_w_in.shape[0]
    depth = norm_mix.shape[0]
    t_real = seq + N_META
    pad_to = ROW_TILE * ATT_TILE // np.gcd(ROW_TILE, ATT_TILE)
    tp = -(-t_real // pad_to) * pad_to

    g_mix = norm_mix.reshape(depth, 1, d)
    g_mlp = norm_mlp.reshape(depth, 1, d)
    w_in = conv_w_in.astype(bf16)
    w_out = conv_w_out.astype(bf16)
    w1 = mlp_w1.astype(bf16)
    w2 = mlp_w2.astype(bf16)
    wk = w_kv[:, :att].astype(bf16)
    wv = w_kv[:, att:].astype(bf16)
    wf = jnp.pad(w_f, ((0, 0), (0, LANES - N_HEADS))).astype(bf16)
    bfp = jnp.pad(b_f, (0, LANES - N_HEADS)).reshape(1, LANES)
    g_kv = norm_kv.reshape(1, d)
    kgain = jnp.tile(k_norm, N_HEADS).reshape(1, att)
    wq = w_q.astype(bf16)
    qgain = (jnp.tile(q_norm, (1, N_HEADS)) * (HEAD_DIM ** -0.5 * LOG2E)).reshape(-1, 1, att)
    wo = w_o.astype(bf16)

    meta = jnp.broadcast_to(meta_tokens[None], (nb, N_META, d))
    xp = jnp.concatenate([meta, x_prompt, jnp.zeros((nb, tp - t_real, d), f32)], axis=1)
    conv_rows = []
    for layer in range(n_a):
        xp, st = _conv_prompt(xp, g_mix, w_in, conv_w, w_out, layer, t_real)
        conv_rows.append(st)
        xp = _mlp(xp.reshape(nb * tp, d), g_mlp, w1, w2, layer).reshape(nb, tp, d)
    k_p, v_p, lf_p, c_p, kp, vt = _kv_side(xp, g_kv, wk, wv, wf, bfp, kgain, prompt=True)
    ct = jnp.swapaxes(c_p[:, :, :N_HEADS], 1, 2).reshape(nb, N_HEADS // HEAD_GROUP, HEAD_GROUP, tp)
    for j in range(depth - n_a):
        layer = n_a + j
        qt = _q_proj(xp, g_mix, wq, qgain, layer, j, transpose=True)
        o = _attn_prompt(qt, ct, kp, vt)
        xp = _mlp(xp.reshape(nb * tp, d), g_mlp, w1, w2, layer,
                  proj=(o.reshape(nb * tp, att), wo, j)).reshape(nb, tp, d)
    y_prompt = xp[:, N_META:t_real]
    k_prompt = k_p[:, :t_real].reshape(nb, t_real, N_HEADS, HEAD_DIM)
    v_prompt = v_p[:, :t_real].reshape(nb, t_real, N_HEADS, HEAD_DIM)
    logf_prompt = lf_p[:, :t_real, :N_HEADS]
    conv_prompt = jnp.stack(conv_rows)

    m_s = db * dseq
    xs = x_sample.reshape(m_s, d)
    conv_rows = []
    for layer in range(n_a):
        xs, st = _conv_sample(xs, state_conv[layer], g_mix, w_in, conv_w, w_out, layer, dseq)
        conv_rows.append(st)
        xs = _mlp(xs, g_mlp, w1, w2, layer)
    k_s, v_s, lf_s = _kv_side(xs.reshape(1, m_s, d), g_kv, wk, wv, wf, bfp, kgain, prompt=False)
    page = cache_k.shape[1]
    cache_kt = jnp.transpose(cache_k, (0, 2, 3, 1))
    cache_vt = jnp.transpose(cache_v, (0, 2, 3, 1))
    cache_lf_t = jnp.swapaxes(cache_logf, 1, 2)
    k_new = k_s.reshape(db, dseq, att)
    v_new = v_s.reshape(db, dseq, att)
    lf_new = lf_s.reshape(db, dseq, LANES)[:, :, :N_HEADS]
    lf_new_t = jnp.pad(jnp.swapaxes(lf_new, 1, 2), ((0, 0), (0, 0), (0, page - dseq)))
    for j in range(depth - n_a):
        layer = n_a + j
        q = _q_proj(xs.reshape(1, m_s, d), g_mix, wq, qgain, layer, j, transpose=False)
        q8 = jnp.pad(q.reshape(db, dseq, att), ((0, 0), (0, 8 - dseq), (0, 0)))
        o8 = _attn_sample(q8, k_new, v_new, lf_new_t, cache_kt, cache_vt, cache_lf_t, page_table, dseq)
        o = o8[:, :dseq].reshape(m_s, att).astype(bf16)
        xs = _mlp(xs, g_mlp, w1, w2, layer, proj=(o, wo, j))
    y_sample = xs.reshape(db, dseq, d)
    k_sample = k_s.reshape(db, dseq, N_HEADS, HEAD_DIM)
    v_sample = v_s.reshape(db, dseq, N_HEADS, HEAD_DIM)
    logf_sample = lf_s.reshape(db, dseq, LANES)[:, :, :N_HEADS]
    conv_sample = jnp.stack(conv_rows)

    return (y_prompt, y_sample, k_prompt, v_prompt, logf_prompt, conv_prompt,
            k_sample, v_sample, logf_sample, conv_sample)
```

```python
import functools

import numpy as np
import jax
import jax.numpy as jnp
from jax import lax
from jax.experimental import pallas as pl
from jax.experimental.pallas import tpu as pltpu

N_HEADS = 16
HEAD_DIM = 64
N_META = 16
CONV_WIDTH = 3
RMS_EPS = 1e-6
NEG_INF = -1e30
LOG2E = 1.4426950408889634

LANES = 128
VMEM_LIMIT = 56 * 1024 * 1024
ROW_TILE = 384
ATT_TILE = 256
KV_ROW_TILE = ATT_TILE
HEAD_GROUP = 4
SAMPLE_PAGES = 8
N_AUG = 6

f32 = jnp.float32
bf16 = jnp.bfloat16


def _cparams(sem):
    return pltpu.CompilerParams(dimension_semantics=sem, vmem_limit_bytes=VMEM_LIMIT)


def _dot(a, b):
    return jnp.dot(a, b, preferred_element_type=f32)


def _dot_nt(a, b):
    return lax.dot_general(a, b, (((1,), (1,)), ((), ())), preferred_element_type=f32)


def _rms(x, g):
    ms = jnp.mean(x * x, axis=-1, keepdims=True)
    return x * lax.rsqrt(ms + RMS_EPS) * g


def _split3(x):
    hi = x.astype(bf16)
    r1 = x - hi.astype(f32)
    mid = r1.astype(bf16)
    lo = (r1 - mid.astype(f32)).astype(bf16)
    return hi, mid, lo


def _dot3(x, w):
    hi, mid, lo = _split3(x)
    return _dot(hi, w) + _dot(mid, w) + _dot(lo, w)


def _dot3_left(w, x):
    hi, mid, lo = _split3(x)
    return _dot(w, hi) + _dot(w, mid) + _dot(w, lo)


def _head_rms(y, seg, seg_t, gain):
    ssq = _dot((y * y).astype(bf16), seg)
    inv = lax.rsqrt(ssq * (1.0 / HEAD_DIM) + RMS_EPS)
    hi = inv.astype(bf16)
    lo = (inv - hi.astype(f32)).astype(bf16)
    inv_full = _dot(hi, seg_t) + _dot(lo, seg_t)
    return y * inv_full * gain


def _conv_kernel(*refs, tm, mode, st_tile, st_row, seq_len):
    if mode == "prompt":
        x_ref, g_ref, win_ref, cw_ref, wout_ref, o_ref, st_ref, ubuf = refs
    else:
        x_ref, s1_ref, s2_ref, g_ref, win_ref, cw_ref, wout_ref, o_ref, st_ref = refs
    x = x_ref[...]
    d = x.shape[1]
    xn = _rms(x, g_ref[...])
    bch = _dot(xn.astype(bf16), win_ref[...])
    b = bch[:, :d]
    u = bch[:, d:2 * d] * bch[:, 2 * d:]
    if mode == "prompt":
        i = pl.program_id(1)

        @pl.when(i == 0)
        def _():
            ubuf[0:8, :] = jnp.zeros((8, d), f32)

        @pl.when(i > 0)
        def _():
            ubuf[0:8, :] = ubuf[tm:tm + 8, :]

        ubuf[8:tm + 8, :] = u
        prev1 = ubuf[7:tm + 7, :]
        prev2 = ubuf[6:tm + 6, :]

        @pl.when(i == st_tile)
        def _():
            st_ref[...] = u[st_row:st_row + 8, :]
    else:
        t = lax.broadcasted_iota(jnp.int32, (tm, d), 0) % seq_len
        prev1 = jnp.where(t >= 1, pltpu.roll(u, 1, 0), s1_ref[...])
        prev2 = jnp.where(t >= 2, pltpu.roll(u, 2, 0), s2_ref[...])
        st_ref[...] = u
    cw = cw_ref[...]
    conv = cw[2:3] * u + cw[0:1] * prev2 + cw[1:2] * prev1
    y = _dot((b * conv).astype(bf16), wout_ref[...])
    o_ref[...] = x + y


def _conv_prompt(x, g, w_in, cw, w_out, layer, t_real):
    n, tp, d = x.shape
    tm = ROW_TILE
    assert tp % tm == 0 and (t_real - 2) % 8 == 6
    st_tile, st_off = divmod(t_real - 2, tm)
    st_row = st_off // 8 * 8
    kern = functools.partial(_conv_kernel, tm=tm, mode="prompt", st_tile=st_tile, st_row=st_row, seq_len=0)
    out, st = pl.pallas_call(
        kern,
        grid=(n, tp // tm),
        in_specs=[
            pl.BlockSpec((None, tm, d), lambda a, i: (a, i, 0)),
            pl.BlockSpec((None, 1, d), lambda a, i: (layer, 0, 0)),
            pl.BlockSpec((None, d, 3 * d), lambda a, i: (layer, 0, 0)),
            pl.BlockSpec((None, CONV_WIDTH, d), lambda a, i: (layer, 0, 0)),
            pl.BlockSpec((None, d, d), lambda a, i: (layer, 0, 0)),
        ],
        out_specs=[
            pl.BlockSpec((None, tm, d), lambda a, i: (a, i, 0)),
            pl.BlockSpec((None, 8, d), lambda a, i: (a, 0, 0)),
        ],
        out_shape=[jax.ShapeDtypeStruct((n, tp, d), f32), jax.ShapeDtypeStruct((n, 8, d), f32)],
        scratch_shapes=[pltpu.VMEM((tm + 8, d), f32)],
        compiler_params=_cparams(("arbitrary", "arbitrary")),
        name="conv_prompt",
    )(x, g, w_in, cw, w_out)
    return out, st[:, 6:8]


def _conv_sample(x, state, g, w_in, cw, w_out, layer, seq_len):
    m, d = x.shape
    nb = m // seq_len
    z = jnp.zeros((nb, seq_len - 1, d), f32)
    s1 = jnp.concatenate([state[:, 1:2], z], axis=1).reshape(m, d)
    s2 = jnp.concatenate([state[:, 0:1], state[:, 1:2], z[:, 1:]], axis=1).reshape(m, d)
    kern = functools.partial(_conv_kernel, tm=m, mode="sample", st_tile=0, st_row=0, seq_len=seq_len)
    full = lambda i: (0, 0)
    out, u = pl.pallas_call(
        kern,
        grid=(1,),
        in_specs=[
            pl.BlockSpec((m, d), full),
            pl.BlockSpec((m, d), full),
            pl.BlockSpec((m, d), full),
            pl.BlockSpec((None, 1, d), lambda i: (layer, 0, 0)),
            pl.BlockSpec((None, d, 3 * d), lambda i: (layer, 0, 0)),
            pl.BlockSpec((None, CONV_WIDTH, d), lambda i: (layer, 0, 0)),
            pl.BlockSpec((None, d, d), lambda i: (layer, 0, 0)),
        ],
        out_specs=[pl.BlockSpec((m, d), full), pl.BlockSpec((m, d), full)],
        out_shape=[jax.ShapeDtypeStruct((m, d), f32), jax.ShapeDtypeStruct((m, d), f32)],
        compiler_params=_cparams(("arbitrary",)),
        name="conv_sample",
    )(x, s1, s2, g, w_in, cw, w_out)
    return out, u.reshape(nb, seq_len, d)[:, seq_len - 2:]


def _mlp_kernel(*refs, has_proj, n_chunks):
    if has_proj:
        x_ref, a_ref, wa_ref, g_ref, w1_ref, w2_ref, o_ref = refs
    else:
        x_ref, g_ref, w1_ref, w2_ref, o_ref = refs
    x = x_ref[...]
    if has_proj:
        x = x + _dot(a_ref[...], wa_ref[...])
    xn = _rms(x, g_ref[...]).astype(bf16)
    ck = w1_ref.shape[1] // n_chunks
    y = x
    for c in range(n_chunks):
        h = _dot(xn, w1_ref[:, c * ck:(c + 1) * ck])
        h = jnp.square(jnp.maximum(h, 0.0)).astype(bf16)
        y = y + _dot(h, w2_ref[c * ck:(c + 1) * ck, :])
    o_ref[...] = y


def _mlp(x, g, w1, w2, layer, proj=None):
    m, d = x.shape
    dff = w1.shape[2]
    tm = ROW_TILE if m % ROW_TILE == 0 else m
    row = lambda i: (i, 0)
    in_specs = [pl.BlockSpec((tm, d), row)]
    args = [x]
    if proj is not None:
        a, wa, pl_idx = proj
        in_specs += [pl.BlockSpec((tm, a.shape[1]), row),
                     pl.BlockSpec((None, a.shape[1], d), lambda i: (pl_idx, 0, 0))]
        args += [a, wa]
    in_specs += [
        pl.BlockSpec((None, 1, d), lambda i: (layer, 0, 0)),
        pl.BlockSpec((None, d, dff), lambda i: (layer, 0, 0)),
        pl.BlockSpec((None, dff, d), lambda i: (layer, 0, 0)),
    ]
    args += [g, w1, w2]
    kern = functools.partial(_mlp_kernel, has_proj=proj is not None, n_chunks=4)
    return pl.pallas_call(
        kern,
        grid=(m // tm,),
        in_specs=in_specs,
        out_specs=pl.BlockSpec((tm, d), row),
        out_shape=jax.ShapeDtypeStruct((m, d), f32),
        compiler_params=_cparams(("arbitrary",)),
        name="mlp",
    )(*args)


def _aug_offset(h):
    return LANES * h + (1 - h % 2) * HEAD_DIM


def _kv_constants():
    att = N_HEADS * HEAD_DIM
    seg = np.zeros((att, LANES), np.float32)
    seg[np.arange(att), np.arange(att) // HEAD_DIM] = 1.0
    pneg = np.zeros((3, LANES, N_HEADS * LANES), np.float32)
    ones_row = np.zeros((1, N_HEADS * LANES), np.float32)
    kmask = np.zeros((1, N_HEADS * LANES), np.float32)
    for h in range(N_HEADS):
        a = _aug_offset(h)
        ones_row[0, a:a + 3] = 1.0
        for part in range(3):
            pneg[part, h, a + 3 + part] = -1.0
        k0 = LANES * h + (h % 2) * HEAD_DIM
        kmask[0, k0:k0 + HEAD_DIM] = 1.0
    return (jnp.asarray(seg, bf16), jnp.asarray(seg.T, bf16), jnp.asarray(pneg, bf16),
            jnp.asarray(ones_row), jnp.asarray(kmask))


def _kv_kernel(*refs, tm, tk, prompt):
    if prompt:
        (x_ref, g_ref, wk_ref, wv_ref, wf_ref, bf_ref, seg_ref, segt_ref, kg_ref,
         ltri_ref, pneg_ref, ones_ref, kmask_ref,
         k_ref, v_ref, lf_ref, c_ref, kp_ref, vt_ref, carry) = refs
    else:
        (x_ref, g_ref, wk_ref, wv_ref, wf_ref, bf_ref, seg_ref, segt_ref, kg_ref,
         k_ref, v_ref, lf_ref) = refs
    xn = _rms(x_ref[...], g_ref[...])
    xb = xn.astype(bf16)
    k = _head_rms(_dot(xb, wk_ref[...]), seg_ref[...], segt_ref[...], kg_ref[...])
    z = _dot(xb, wf_ref[...]) + bf_ref[...]
    lane = lax.broadcasted_iota(jnp.int32, z.shape, 1)
    logf = jnp.where(lane < N_HEADS, jax.nn.log_sigmoid(z), 0.0)
    lf_ref[...] = logf
    if not prompt:
        k_ref[...] = k
        v_ref[...] = _dot(xb, wv_ref[...])
        return
    k_ref[...] = k.T
    v_t = _dot_nt(wv_ref[...], xb)
    v_ref[...] = v_t

    @pl.when(pl.program_id(1) == 0)
    def _():
        carry[...] = jnp.zeros(carry.shape, f32)

    c = _dot3_left(ltri_ref[...], logf) + carry[0:1, :]
    carry[...] = jnp.broadcast_to(c[tm - 1:tm, :], carry.shape)
    c2 = c * LOG2E
    c_ref[...] = c2
    hi, mid, lo = _split3(c2)
    aug = _dot(hi, pneg_ref[0]) + _dot(mid, pneg_ref[1]) + _dot(lo, pneg_ref[2]) + ones_ref[...]
    pairs = []
    for p in range(N_HEADS // 2):
        blk = k[:, p * LANES:(p + 1) * LANES]
        pairs += [blk, blk]
    k_rep = jnp.concatenate(pairs, axis=1)
    kp_ref[...] = jnp.where(kmask_ref[...] > 0.5, k_rep, aug).astype(bf16)
    vt = v_t.astype(bf16)
    for j in range(tm // tk):
        vt_ref[j] = vt[:, j * tk:(j + 1) * tk]


def _kv_side(x, g, wk, wv, wf, bfp, kgain, prompt, t_out=None):
    n, t, d = x.shape
    att = N_HEADS * HEAD_DIM
    tm = KV_ROW_TILE if prompt else t
    tk = ATT_TILE
    assert t % tm == 0
    seg, seg_t, pneg, ones_row, kmask = _kv_constants()
    const = lambda shape: pl.BlockSpec(shape, lambda a, i: (0,) * len(shape))
    row = lambda w: pl.BlockSpec((None, tm, w), lambda a, i: (a, i, 0))
    in_specs = [row(d), const((1, d)), const((d, att)), const((d, att)), const((d, LANES)),
                const((1, LANES)), const((att, LANES)), const((LANES, att)), const((1, att))]
    args = [x, g, wk, wv, wf, bfp, seg, seg_t, kgain]
    if prompt:
        assert pl.cdiv(t_out, tm) == t // tm
        kv_spec = pl.BlockSpec((None, att, tm), lambda a, i: (a, 0, i))
        kv_shape = jax.ShapeDtypeStruct((n, att, t_out), f32)
    else:
        kv_spec, kv_shape = row(att), jax.ShapeDtypeStruct((n, t, att), f32)
    out_specs = [kv_spec, kv_spec, row(LANES)]
    out_shape = [kv_shape, kv_shape, jax.ShapeDtypeStruct((n, t, LANES), f32)]
    scratch = []
    if prompt:
        ltri = jnp.asarray(np.tril(np.ones((tm, tm), np.float32)), bf16)
        in_specs += [const((tm, tm)), const((3, LANES, N_HEADS * LANES)),
                     const((1, N_HEADS * LANES)), const((1, N_HEADS * LANES))]
        args += [ltri, pneg, ones_row, kmask]
        out_specs += [row(LANES), row(N_HEADS * LANES),
                      pl.BlockSpec((None, tm // tk, att, tk), lambda a, i: (a, i, 0, 0))]
        out_shape += [jax.ShapeDtypeStruct((n, t, LANES), f32),
                      jax.ShapeDtypeStruct((n, t, N_HEADS * LANES), bf16),
                      jax.ShapeDtypeStruct((n, t // tk, att, tk), bf16)]
        scratch = [pltpu.VMEM((8, LANES), f32)]
    kern = functools.partial(_kv_kernel, tm=tm, tk=tk, prompt=prompt)
    return pl.pallas_call(
        kern,
        grid=(n, t // tm),
        in_specs=in_specs,
        out_specs=out_specs,
        out_shape=out_shape,
        scratch_shapes=scratch,
        compiler_params=_cparams(("arbitrary", "arbitrary")),
        name="kv_prompt" if prompt else "kv_sample",
    )(*args)


def _q_kernel(x_ref, g_ref, wq_ref, seg_ref, segt_ref, qg_ref, o_ref, *, transpose):
    xn = _rms(x_ref[...], g_ref[...])
    q = _head_rms(_dot(xn.astype(bf16), wq_ref[...]), seg_ref[...], segt_ref[...], qg_ref[...])
    if transpose:
        o_ref[...] = q.T.astype(bf16)
    else:
        o_ref[...] = q


def _q_proj(x, g, wq, qgain, layer, j, transpose):
    n, t, d = x.shape
    att = N_HEADS * HEAD_DIM
    tm = ROW_TILE if t % ROW_TILE == 0 else t
    seg, seg_t = _kv_constants()[:2]
    const = lambda shape: pl.BlockSpec(shape, lambda a, i: (0,) * len(shape))
    if transpose:
        out_spec = pl.BlockSpec((None, att, tm), lambda a, i: (a, 0, i))
        out_shape = jax.ShapeDtypeStruct((n, att, t), bf16)
    else:
        out_spec = pl.BlockSpec((None, tm, att), lambda a, i: (a, i, 0))
        out_shape = jax.ShapeDtypeStruct((n, t, att), f32)
    return pl.pallas_call(
        functools.partial(_q_kernel, transpose=transpose),
        grid=(n, t // tm),
        in_specs=[
            pl.BlockSpec((None, tm, d), lambda a, i: (a, i, 0)),
            pl.BlockSpec((None, 1, d), lambda a, i: (layer, 0, 0)),
            pl.BlockSpec((None, d, att), lambda a, i: (j, 0, 0)),
            const((att, LANES)), const((LANES, att)),
            pl.BlockSpec((None, 1, att), lambda a, i: (j, 0, 0)),
        ],
        out_specs=out_spec,
        out_shape=out_shape,
        compiler_params=_cparams(("arbitrary", "arbitrary")),
        name="q_proj",
    )(x, g, wq, seg, seg_t, qgain)


def _attn_prompt_kernel(qt_ref, ct_ref, kp_ref, vt_ref, o_ref, rhs_sc, s_sc, m_sc, acc_sc, *, t, hg):
    i = pl.program_id(2)
    row = lax.broadcasted_iota(jnp.int32, (HEAD_DIM, t), 0)
    for e in range(hg):
        hi, mid, lo = _split3(ct_ref[e:e + 1, :])
        aug = jnp.where(row == 0, hi.astype(f32),
                        jnp.where(row == 1, mid.astype(f32),
                                  jnp.where(row == 2, lo.astype(f32),
                                            jnp.where(row < N_AUG, 1.0, 0.0)))).astype(bf16)
        qe = qt_ref[e * HEAD_DIM:(e + 1) * HEAD_DIM, :]
        rhs_sc[e] = jnp.concatenate([qe, aug] if e % 2 == 0 else [aug, qe], axis=0)
        m_sc[e] = jnp.full((1, t), NEG_INF, f32)
        acc_sc[e] = jnp.zeros((HEAD_DIM + 16, t), f32)
    ones_blk = (lax.broadcasted_iota(jnp.int32, (16, t), 0) == 0).astype(bf16)
    causal = lax.broadcasted_iota(jnp.int32, (t, t), 0) <= lax.broadcasted_iota(jnp.int32, (t, t), 1)

    def scores(j, slot):
        off = pl.multiple_of(j * t, t)
        for e in range(hg):
            kb = kp_ref[pl.ds(off, t), e * LANES:(e + 1) * LANES]
            s_sc[slot, e] = _dot(kb, rhs_sc[e])

    def update(j, slot, masked):
        for e in range(hg):
            s = s_sc[slot, e]
            if masked:
                s = jnp.where(causal, s, NEG_INF)
            m = m_sc[e]
            m_new = jnp.maximum(m, jnp.max(s, axis=0, keepdims=True))
            p = jnp.exp2(s - m_new).astype(bf16)
            alpha = jnp.exp2(m - m_new)
            lhs = jnp.concatenate([vt_ref[j, e * HEAD_DIM:(e + 1) * HEAD_DIM, :], ones_blk], axis=0)
            acc_sc[e] = alpha * acc_sc[e] + _dot(lhs, p)
            m_sc[e] = m_new

    scores(0, 0)

    def pair(u, c):
        j = 2 * u
        scores(j + 1, 1)
        update(j, 0, False)
        scores(j + 2, 0)
        update(j + 1, 1, False)
        return c

    lax.fori_loop(0, lax.shift_right_logical(i, 1), pair, 0)

    @pl.when(i % 2 == 0)
    def _():
        update(i, 0, True)

    @pl.when(i % 2 == 1)
    def _():
        scores(i, 1)
        update(i - 1, 0, False)
        update(i, 1, True)

    outs = []
    for e in range(hg):
        acc = acc_sc[e]
        outs.append(acc[:HEAD_DIM] * pl.reciprocal(acc[HEAD_DIM:HEAD_DIM + 1], approx=False))
    o_ref[...] = jnp.concatenate(outs, axis=0).T.astype(bf16)


def _attn_prompt(qt, ct, kp, vt):
    n, att, tlen = qt.shape
    t, hg = ATT_TILE, HEAD_GROUP
    kern = functools.partial(_attn_prompt_kernel, t=t, hg=hg)
    return pl.pallas_call(
        kern,
        grid=(n, N_HEADS // hg, tlen // t),
        in_specs=[
            pl.BlockSpec((None, hg * HEAD_DIM, t), lambda a, g, i: (a, g, i)),
            pl.BlockSpec((None, None, hg, t), lambda a, g, i: (a, g, 0, i)),
            pl.BlockSpec((None, tlen, hg * LANES), lambda a, g, i: (a, 0, g)),
            pl.BlockSpec((None, tlen // t, hg * HEAD_DIM, t), lambda a, g, i: (a, 0, g, 0)),
        ],
        out_specs=pl.BlockSpec((None, t, hg * HEAD_DIM), lambda a, g, i: (a, i, g)),
        out_shape=jax.ShapeDtypeStruct((n, tlen, att), bf16),
        scratch_shapes=[pltpu.VMEM((hg, LANES, t), bf16), pltpu.VMEM((2, hg, t, t), f32),
                        pltpu.VMEM((hg, 1, t), f32), pltpu.VMEM((hg, HEAD_DIM + 16, t), f32)],
        compiler_params=_cparams(("arbitrary", "arbitrary", "arbitrary")),
        name="attn_prompt",
    )(qt, ct, kp, vt)


def _attn_sample_kernel(*refs, n_steps, pps, page, seq_len):
    pt_ref, q_ref, knew_ref, vnew_ref, lfnew_ref = refs[:5]
    ck_refs = refs[5:5 + pps]
    cv_refs = refs[5 + pps:5 + 2 * pps]
    clf_refs = refs[5 + 2 * pps:5 + 3 * pps]
    usuf_ref, uinc_ref, o_ref, qbd_sc, m_sc, l_sc, acc_sc, carry_sc, e_sc, kpage, vpage = refs[5 + 3 * pps:]
    b = pl.program_id(0)
    j = pl.program_id(1)
    nq = seq_len * N_HEADS
    att = N_HEADS * HEAD_DIM

    def tile_t(x):
        return jnp.concatenate([x] * seq_len, axis=0)

    def softmax_update(s, pv_fn):
        m_prev = m_sc[...]
        m_new = jnp.maximum(m_prev, jnp.max(s, axis=1, keepdims=True))
        p = jnp.exp2(s - m_new)
        alpha = jnp.exp2(m_prev - m_new)
        l_sc[...] = alpha * l_sc[...] + jnp.sum(p, axis=1, keepdims=True)
        acc_sc[...] = alpha * acc_sc[...] + pv_fn(p.astype(bf16))
        m_sc[...] = m_new

    @pl.when(jnp.logical_and(b == 0, j == 0))
    def _():
        kpage[...] = jnp.zeros(kpage.shape, f32)
        vpage[...] = jnp.zeros(vpage.shape, f32)

    @pl.when(j == 0)
    def _():
        q = q_ref[...]
        own = (lax.broadcasted_iota(jnp.int32, (N_HEADS, att), 1) // HEAD_DIM
               == lax.broadcasted_iota(jnp.int32, (N_HEADS, att), 0))
        qbd_sc[...] = jnp.concatenate(
            [jnp.where(own, jnp.broadcast_to(q[t:t + 1], (N_HEADS, att)), 0.0) for t in range(seq_len)],
            axis=0).astype(bf16)
        e_t = _dot3(lfnew_ref[...], uinc_ref[...]) * LOG2E
        e_col = jnp.concatenate([e_t[:, t:t + 1] for t in range(seq_len)], axis=0)
        e_sc[...] = e_col
        carry_sc[...] = jnp.zeros(carry_sc.shape, f32)
        m_sc[...] = jnp.full(m_sc.shape, NEG_INF, f32)
        l_sc[...] = jnp.zeros(l_sc.shape, f32)
        acc_sc[...] = jnp.zeros(acc_sc.shape, f32)
        kpage[0:seq_len, :] = knew_ref[...]
        vpage[0:seq_len, :] = vnew_ref[...]
        s = _dot_nt(qbd_sc[...], kpage[...].astype(bf16))
        lane = lax.broadcasted_iota(jnp.int32, (nq, page), 1)
        t_row = lax.broadcasted_iota(jnp.int32, (nq, page), 0) // N_HEADS
        valid = jnp.logical_and(lane <= t_row, lane < seq_len)
        s = jnp.where(valid, s + (e_col - tile_t(e_t)), NEG_INF)
        softmax_update(s, lambda p: _dot(p, vpage[...].astype(bf16)))

    d_pages = []
    for r in range(pps):
        dd = _dot3(clf_refs[r][...], usuf_ref[...])
        d_pages.append((dd[:, :page] + carry_sc[...]) * LOG2E)
        carry_sc[...] = carry_sc[...] + dd[:, page:]
    kt = jnp.concatenate([ck_refs[r][...].reshape(att, page) for r in range(pps)], axis=1).astype(bf16)
    vt = jnp.concatenate([cv_refs[r][...].reshape(att, page) for r in range(pps)], axis=1).astype(bf16)
    s = _dot(qbd_sc[...], kt) + tile_t(jnp.concatenate(d_pages, axis=1)) + e_sc[...]
    softmax_update(s, lambda p: _dot_nt(p, vt))

    @pl.when(j == n_steps - 1)
    def _():
        o = acc_sc[...] * pl.reciprocal(l_sc[...], approx=False)
        own = (lax.broadcasted_iota(jnp.int32, (nq, att), 1) // HEAD_DIM
               == lax.broadcasted_iota(jnp.int32, (nq, att), 0) % N_HEADS)
        o = jnp.where(own, o, 0.0)
        o_ref[...] = jnp.concatenate(
            [jnp.sum(o[t * N_HEADS:(t + 1) * N_HEADS], axis=0, keepdims=True) for t in range(seq_len)], axis=0)


def _attn_sample(q, k_new, v_new, lf_new_t, cache_kt, cache_vt, cache_lf_t, page_table, seq_len):
    nb = q.shape[0]
    att = N_HEADS * HEAD_DIM
    n_pages = page_table.shape[1]
    page = cache_lf_t.shape[2]
    pps = SAMPLE_PAGES if n_pages % SAMPLE_PAGES == 0 else 1
    n_steps = n_pages // pps
    assert page == LANES
    usuf = np.concatenate([np.triu(np.ones((page, page), np.float32), 1).T, np.ones((page, page), np.float32)], axis=1)
    uinc = np.triu(np.ones((page, page), np.float32))
    pt = page_table.reshape(-1).astype(jnp.int32)
    seq = lambda shape: pl.BlockSpec((None,) + shape, lambda b, j, pt: (b,) + (0,) * len(shape))

    def pool(shape, r):
        return pl.BlockSpec((None,) + shape,
                            lambda b, j, pt: (pt[b * n_pages + n_pages - 1 - (j * pps + r)],) + (0,) * len(shape))

    const = lambda shape: pl.BlockSpec(shape, lambda b, j, pt: (0, 0))
    nq = seq_len * N_HEADS
    kern = functools.partial(_attn_sample_kernel, n_steps=n_steps, pps=pps, page=page, seq_len=seq_len)
    in_specs = [seq((seq_len, att)), seq((seq_len, att)), seq((seq_len, att)), seq((N_HEADS, page))]
    in_specs += [pool((N_HEADS, HEAD_DIM, page), r) for r in range(pps)]
    in_specs += [pool((N_HEADS, HEAD_DIM, page), r) for r in range(pps)]
    in_specs += [pool((N_HEADS, page), r) for r in range(pps)]
    in_specs += [const((page, 2 * page)), const((page, page))]
    return pl.pallas_call(
        kern,
        grid_spec=pltpu.PrefetchScalarGridSpec(
            num_scalar_prefetch=1,
            grid=(nb, n_steps),
            in_specs=in_specs,
            out_specs=seq((seq_len, att)),
            scratch_shapes=[pltpu.VMEM((nq, att), bf16),
                            pltpu.VMEM((nq, 1), f32), pltpu.VMEM((nq, 1), f32), pltpu.VMEM((nq, att), f32),
                            pltpu.VMEM((N_HEADS, page), f32), pltpu.VMEM((nq, 1), f32),
                            pltpu.VMEM((page, att), f32), pltpu.VMEM((page, att), f32)],
        ),
        out_shape=jax.ShapeDtypeStruct((nb, seq_len, att), f32),
        compiler_params=_cparams(("arbitrary", "arbitrary")),
        name="attn_sample",
    )(pt, q, k_new, v_new, lf_new_t, *([cache_kt] * pps), *([cache_vt] * pps), *([cache_lf_t] * pps),
      jnp.asarray(usuf, bf16), jnp.asarray(uinc, bf16))


def kernel(x_prompt, x_sample, cache_k, cache_v, cache_logf, state_conv, page_table, meta_tokens, norm_mix, norm_mlp, conv_w_in, conv_w, conv_w_out, norm_kv, w_kv, k_norm, w_f, b_f, w_q, q_norm, w_o, mlp_w1, mlp_w2):
    nb, seq, d = x_prompt.shape
    db, dseq, _ = x_sample.shape
    att = N_HEADS * HEAD_DIM
    n_a = conv_w_in.shape[0]
    depth = norm_mix.shape[0]
    t_real = seq + N_META
    pad_to = ROW_TILE * ATT_TILE // np.gcd(ROW_TILE, ATT_TILE)
    tp = -(-t_real // pad_to) * pad_to

    g_mix = norm_mix.reshape(depth, 1, d)
    g_mlp = norm_mlp.reshape(depth, 1, d)
    w_in = conv_w_in.astype(bf16)
    w_out = conv_w_out.astype(bf16)
    w1 = mlp_w1.astype(bf16)
    w2 = mlp_w2.astype(bf16)
    wk = w_kv[:, :att].astype(bf16)
    wv = w_kv[:, att:].astype(bf16)
    wv_t = w_kv[:, att:].T.astype(bf16)
    wf = jnp.pad(w_f, ((0, 0), (0, LANES - N_HEADS))).astype(bf16)
    bfp = jnp.pad(b_f, (0, LANES - N_HEADS)).reshape(1, LANES)
    g_kv = norm_kv.reshape(1, d)
    kgain = jnp.tile(k_norm, N_HEADS).reshape(1, att)
    wq = w_q.astype(bf16)
    qgain = (jnp.tile(q_norm, (1, N_HEADS)) * (HEAD_DIM ** -0.5 * LOG2E)).reshape(-1, 1, att)
    wo = w_o.astype(bf16)

    meta = jnp.broadcast_to(meta_tokens[None], (nb, N_META, d))
    xp = jnp.concatenate([meta, x_prompt, jnp.zeros((nb, tp - t_real, d), f32)], axis=1)
    conv_rows = []
    for layer in range(n_a):
        xp, st = _conv_prompt(xp, g_mix, w_in, conv_w, w_out, layer, t_real)
        conv_rows.append(st)
        xp = _mlp(xp.reshape(nb * tp, d), g_mlp, w1, w2, layer).reshape(nb, tp, d)
    k_p, v_p, lf_p, c_p, kp, vt = _kv_side(xp, g_kv, wk, wv_t, wf, bfp, kgain, prompt=True, t_out=t_real)
    ct = jnp.swapaxes(c_p[:, :, :N_HEADS], 1, 2).reshape(nb, N_HEADS // HEAD_GROUP, HEAD_GROUP, tp)
    for j in range(depth - n_a):
        layer = n_a + j
        qt = _q_proj(xp, g_mix, wq, qgain, layer, j, transpose=True)
        o = _attn_prompt(qt, ct, kp, vt)
        xp = _mlp(xp.reshape(nb * tp, d), g_mlp, w1, w2, layer,
                  proj=(o.reshape(nb * tp, att), wo, j)).reshape(nb, tp, d)
    y_prompt = xp[:, N_META:t_real]
    k_prompt = jnp.transpose(k_p.reshape(nb, N_HEADS, HEAD_DIM, t_real), (0, 3, 1, 2))
    v_prompt = jnp.transpose(v_p.reshape(nb, N_HEADS, HEAD_DIM, t_real), (0, 3, 1, 2))
    logf_prompt = lf_p[:, :t_real, :N_HEADS]
    conv_prompt = jnp.stack(conv_rows)

    m_s = db * dseq
    xs = x_sample.reshape(m_s, d)
    conv_rows = []
    for layer in range(n_a):
        xs, st = _conv_sample(xs, state_conv[layer], g_mix, w_in, conv_w, w_out, layer, dseq)
        conv_rows.append(st)
        xs = _mlp(xs, g_mlp, w1, w2, layer)
    k_s, v_s, lf_s = _kv_side(xs.reshape(1, m_s, d), g_kv, wk, wv, wf, bfp, kgain, prompt=False)
    page = cache_k.shape[1]
    cache_kt = jnp.transpose(cache_k, (0, 2, 3, 1))
    cache_vt = jnp.transpose(cache_v, (0, 2, 3, 1))
    cache_lf_t = jnp.swapaxes(cache_logf, 1, 2)
    k_new = k_s.reshape(db, dseq, att)
    v_new = v_s.reshape(db, dseq, att)
    lf_new = lf_s.reshape(db, dseq, LANES)[:, :, :N_HEADS]
    lf_new_t = jnp.pad(jnp.swapaxes(lf_new, 1, 2), ((0, 0), (0, 0), (0, page - dseq)))
    for j in range(depth - n_a):
        layer = n_a + j
        q = _q_proj(xs.reshape(1, m_s, d), g_mix, wq, qgain, layer, j, transpose=False)
        o = _attn_sample(q.reshape(db, dseq, att), k_new, v_new, lf_new_t, cache_kt, cache_vt, cache_lf_t,
                         page_table, dseq)
        o = o.reshape(m_s, att).astype(bf16)
        xs = _mlp(xs, g_mlp, w1, w2, layer, proj=(o, wo, j))
    y_sample = xs.reshape(db, dseq, d)
    k_sample = k_s.reshape(db, dseq, N_HEADS, HEAD_DIM)
    v_sample = v_s.reshape(db, dseq, N_HEADS, HEAD_DIM)
    logf_sample = lf_s.reshape(db, dseq, LANES)[:, :, :N_HEADS]
    conv_sample = jnp.stack(conv_rows)

    return (y_prompt, y_sample, k_prompt, v_prompt, logf_prompt, conv_prompt,
            k_sample, v_sample, logf_sample, conv_sample)
```

```python
import functools

import numpy as np
import jax
import jax.numpy as jnp
from jax import lax
from jax.experimental import pallas as pl
from jax.experimental.pallas import tpu as pltpu

N_HEADS = 16
HEAD_DIM = 64
N_META = 16
CONV_WIDTH = 3
RMS_EPS = 1e-6
NEG_INF = -1e30
LOG2E = 1.4426950408889634

LANES = 128
VMEM_LIMIT = 56 * 1024 * 1024
ROW_TILE = 384
MLP_ROW_TILE = 2 * ROW_TILE
ATT_TILE = 256
KV_ROW_TILE = ATT_TILE
HEAD_GROUP = 4
SAMPLE_PAGES = 8
N_AUG = 6

f32 = jnp.float32
bf16 = jnp.bfloat16


def _cparams(sem):
    return pltpu.CompilerParams(dimension_semantics=sem, vmem_limit_bytes=VMEM_LIMIT)


def _dot(a, b):
    return jnp.dot(a, b, preferred_element_type=f32)


def _dot_nt(a, b):
    return lax.dot_general(a, b, (((1,), (1,)), ((), ())), preferred_element_type=f32)


def _rms(x, g):
    ms = jnp.mean(x * x, axis=-1, keepdims=True)
    return x * lax.rsqrt(ms + RMS_EPS) * g


def _split3(x):
    hi = x.astype(bf16)
    r1 = x - hi.astype(f32)
    mid = r1.astype(bf16)
    lo = (r1 - mid.astype(f32)).astype(bf16)
    return hi, mid, lo


def _dot3(x, w):
    hi, mid, lo = _split3(x)
    return _dot(hi, w) + _dot(mid, w) + _dot(lo, w)


def _dot3_left(w, x):
    hi, mid, lo = _split3(x)
    return _dot(w, hi) + _dot(w, mid) + _dot(w, lo)


def _head_rms(y, seg, seg_t, gain):
    ssq = _dot((y * y).astype(bf16), seg)
    inv = lax.rsqrt(ssq * (1.0 / HEAD_DIM) + RMS_EPS)
    hi = inv.astype(bf16)
    lo = (inv - hi.astype(f32)).astype(bf16)
    lane = lax.broadcasted_iota(jnp.int32, inv.shape, 1)
    inv_full = _dot(jnp.where(lane < N_HEADS, hi, lo), seg_t)
    return y * inv_full * gain


def _conv_kernel(*refs, tm, mode, st_tile, st_row, seq_len):
    if mode == "prompt":
        x_ref, g_ref, win_ref, cw_ref, wout_ref, o_ref, st_ref, ubuf = refs
    else:
        x_ref, s1_ref, s2_ref, g_ref, win_ref, cw_ref, wout_ref, o_ref, st_ref = refs
    x = x_ref[...]
    d = x.shape[1]
    xn = _rms(x, g_ref[...])
    bch = _dot(xn.astype(bf16), win_ref[...])
    b = bch[:, :d]
    u = bch[:, d:2 * d] * bch[:, 2 * d:]
    if mode == "prompt":
        i = pl.program_id(1)

        @pl.when(i == 0)
        def _():
            ubuf[0:8, :] = jnp.zeros((8, d), f32)

        @pl.when(i > 0)
        def _():
            ubuf[0:8, :] = ubuf[tm:tm + 8, :]

        ubuf[8:tm + 8, :] = u
        prev1 = ubuf[7:tm + 7, :]
        prev2 = ubuf[6:tm + 6, :]

        @pl.when(i == st_tile)
        def _():
            st_ref[...] = u[st_row:st_row + 8, :]
    else:
        t = lax.broadcasted_iota(jnp.int32, (tm, d), 0) % seq_len
        prev1 = jnp.where(t >= 1, pltpu.roll(u, 1, 0), s1_ref[...])
        prev2 = jnp.where(t >= 2, pltpu.roll(u, 2, 0), s2_ref[...])
        st_ref[...] = u
    cw = cw_ref[...]
    conv = cw[2:3] * u + cw[0:1] * prev2 + cw[1:2] * prev1
    y = _dot((b * conv).astype(bf16), wout_ref[...])
    o_ref[...] = x + y


def _conv_prompt(x, g, w_in, cw, w_out, layer, t_real):
    n, tp, d = x.shape
    tm = ROW_TILE
    assert tp % tm == 0 and (t_real - 2) % 8 == 6
    st_tile, st_off = divmod(t_real - 2, tm)
    st_row = st_off // 8 * 8
    kern = functools.partial(_conv_kernel, tm=tm, mode="prompt", st_tile=st_tile, st_row=st_row, seq_len=0)
    out, st = pl.pallas_call(
        kern,
        grid=(n, tp // tm),
        in_specs=[
            pl.BlockSpec((None, tm, d), lambda a, i: (a, i, 0)),
            pl.BlockSpec((None, 1, d), lambda a, i: (layer, 0, 0)),
            pl.BlockSpec((None, d, 3 * d), lambda a, i: (layer, 0, 0)),
            pl.BlockSpec((None, CONV_WIDTH, d), lambda a, i: (layer, 0, 0)),
            pl.BlockSpec((None, d, d), lambda a, i: (layer, 0, 0)),
        ],
        out_specs=[
            pl.BlockSpec((None, tm, d), lambda a, i: (a, i, 0)),
            pl.BlockSpec((None, 8, d), lambda a, i: (a, 0, 0)),
        ],
        out_shape=[jax.ShapeDtypeStruct((n, tp, d), f32), jax.ShapeDtypeStruct((n, 8, d), f32)],
        scratch_shapes=[pltpu.VMEM((tm + 8, d), f32)],
        compiler_params=_cparams(("arbitrary", "arbitrary")),
        name="conv_prompt",
    )(x, g, w_in, cw, w_out)
    return out, st[:, 6:8]


def _conv_sample(x, state, g, w_in, cw, w_out, layer, seq_len):
    m, d = x.shape
    nb = m // seq_len
    z = jnp.zeros((nb, seq_len - 1, d), f32)
    s1 = jnp.concatenate([state[:, 1:2], z], axis=1).reshape(m, d)
    s2 = jnp.concatenate([state[:, 0:1], state[:, 1:2], z[:, 1:]], axis=1).reshape(m, d)
    kern = functools.partial(_conv_kernel, tm=m, mode="sample", st_tile=0, st_row=0, seq_len=seq_len)
    full = lambda i: (0, 0)
    out, u = pl.pallas_call(
        kern,
        grid=(1,),
        in_specs=[
            pl.BlockSpec((m, d), full),
            pl.BlockSpec((m, d), full),
            pl.BlockSpec((m, d), full),
            pl.BlockSpec((None, 1, d), lambda i: (layer, 0, 0)),
            pl.BlockSpec((None, d, 3 * d), lambda i: (layer, 0, 0)),
            pl.BlockSpec((None, CONV_WIDTH, d), lambda i: (layer, 0, 0)),
            pl.BlockSpec((None, d, d), lambda i: (layer, 0, 0)),
        ],
        out_specs=[pl.BlockSpec((m, d), full), pl.BlockSpec((m, d), full)],
        out_shape=[jax.ShapeDtypeStruct((m, d), f32), jax.ShapeDtypeStruct((m, d), f32)],
        compiler_params=_cparams(("arbitrary",)),
        name="conv_sample",
    )(x, s1, s2, g, w_in, cw, w_out)
    return out, u.reshape(nb, seq_len, d)[:, seq_len - 2:]


def _mlp_kernel(*refs, has_proj, n_chunks):
    if has_proj:
        x_ref, a_ref, wa_ref, g_ref, w1_ref, w2_ref, o_ref = refs
    else:
        x_ref, g_ref, w1_ref, w2_ref, o_ref = refs
    x = x_ref[...]
    if has_proj:
        x = x + _dot(a_ref[...], wa_ref[...])
    xn = _rms(x, g_ref[...]).astype(bf16)
    ck = w1_ref.shape[1] // n_chunks
    y = x
    for c in range(n_chunks):
        h = _dot(xn, w1_ref[:, c * ck:(c + 1) * ck])
        h = jnp.square(jnp.maximum(h, 0.0)).astype(bf16)
        y = y + _dot(h, w2_ref[c * ck:(c + 1) * ck, :])
    o_ref[...] = y


def _mlp(x, g, w1, w2, layer, proj=None):
    m, d = x.shape
    dff = w1.shape[2]
    tm = MLP_ROW_TILE if m % MLP_ROW_TILE == 0 else m
    row = lambda i: (i, 0)
    in_specs = [pl.BlockSpec((tm, d), row)]
    args = [x]
    if proj is not None:
        a, wa, pl_idx = proj
        in_specs += [pl.BlockSpec((tm, a.shape[1]), row),
                     pl.BlockSpec((None, a.shape[1], d), lambda i: (pl_idx, 0, 0))]
        args += [a, wa]
    in_specs += [
        pl.BlockSpec((None, 1, d), lambda i: (layer, 0, 0)),
        pl.BlockSpec((None, d, dff), lambda i: (layer, 0, 0)),
        pl.BlockSpec((None, dff, d), lambda i: (layer, 0, 0)),
    ]
    args += [g, w1, w2]
    kern = functools.partial(_mlp_kernel, has_proj=proj is not None, n_chunks=4)
    return pl.pallas_call(
        kern,
        grid=(m // tm,),
        in_specs=in_specs,
        out_specs=pl.BlockSpec((tm, d), row),
        out_shape=jax.ShapeDtypeStruct((m, d), f32),
        compiler_params=_cparams(("arbitrary",)),
        name="mlp",
    )(*args)


def _aug_offset(h):
    return LANES * h + (1 - h % 2) * HEAD_DIM


def _kv_constants():
    att = N_HEADS * HEAD_DIM
    seg = np.zeros((att, LANES), np.float32)
    seg[np.arange(att), np.arange(att) // HEAD_DIM] = 1.0
    seg[np.arange(att), N_HEADS + np.arange(att) // HEAD_DIM] = 1.0
    pneg = np.zeros((LANES, N_HEADS * LANES), np.float32)
    ones_row = np.zeros((1, N_HEADS * LANES), np.float32)
    kmask = np.zeros((1, N_HEADS * LANES), np.float32)
    for h in range(N_HEADS):
        a = _aug_offset(h)
        ones_row[0, a:a + 3] = 1.0
        for part in range(3):
            pneg[part * N_HEADS + h, a + 3 + part] = -1.0
        k0 = LANES * h + (h % 2) * HEAD_DIM
        kmask[0, k0:k0 + HEAD_DIM] = 1.0
    return (jnp.asarray(seg, bf16), jnp.asarray(seg.T, bf16), jnp.asarray(pneg, bf16),
            jnp.asarray(ones_row), jnp.asarray(kmask))


def _kv_kernel(*refs, tm, tk, prompt):
    if prompt:
        (x_ref, g_ref, wk_ref, wv_ref, wf_ref, bf_ref, seg_ref, segt_ref, kg_ref,
         ltri_ref, pneg_ref, ones_ref, kmask_ref,
         k_ref, v_ref, lf_ref, c_ref, kp_ref, vt_ref, carry) = refs
    else:
        (x_ref, g_ref, wk_ref, wv_ref, wf_ref, bf_ref, seg_ref, segt_ref, kg_ref,
         k_ref, v_ref, lf_ref) = refs
    xn = _rms(x_ref[...], g_ref[...])
    xb = xn.astype(bf16)
    k = _head_rms(_dot(xb, wk_ref[...]), seg_ref[...], segt_ref[...], kg_ref[...])
    z = _dot(xb, wf_ref[...]) + bf_ref[...]
    lane = lax.broadcasted_iota(jnp.int32, z.shape, 1)
    logf = jnp.where(lane < 3 * N_HEADS, jax.nn.log_sigmoid(z), 0.0)
    lf_ref[...] = logf
    if not prompt:
        k_ref[...] = k
        v_ref[...] = _dot(xb, wv_ref[...])
        return
    k_ref[...] = k.T
    v_t = _dot_nt(wv_ref[...], xb)
    v_ref[...] = v_t

    @pl.when(pl.program_id(1) == 0)
    def _():
        carry[...] = jnp.zeros(carry.shape, f32)

    c = _dot3_left(ltri_ref[...], logf) + carry[0:1, :]
    carry[...] = jnp.broadcast_to(c[tm - 1:tm, :], carry.shape)
    c2 = c * LOG2E
    c_ref[...] = c2
    hi, mid, lo = _split3(c2)
    pieces = jnp.where(lane < N_HEADS, hi, jnp.where(lane < 2 * N_HEADS, mid, lo))
    aug = _dot(pieces, pneg_ref[...]) + ones_ref[...]
    pairs = []
    for p in range(N_HEADS // 2):
        blk = k[:, p * LANES:(p + 1) * LANES]
        pairs += [blk, blk]
    k_rep = jnp.concatenate(pairs, axis=1)
    kp = jnp.where(kmask_ref[...] > 0.5, k_rep, aug).astype(bf16)
    for h in range(N_HEADS):
        kp_ref[h] = kp[:, h * LANES:(h + 1) * LANES]
    vt = v_t.astype(bf16)
    for j in range(tm // tk):
        vt_ref[j] = vt[:, j * tk:(j + 1) * tk]


def _kv_side(x, g, wk, wv, wf, bfp, kgain, prompt, t_out=None):
    n, t, d = x.shape
    att = N_HEADS * HEAD_DIM
    tm = KV_ROW_TILE if prompt else t
    tk = ATT_TILE
    assert t % tm == 0
    seg, seg_t, pneg, ones_row, kmask = _kv_constants()
    const = lambda shape: pl.BlockSpec(shape, lambda a, i: (0,) * len(shape))
    row = lambda w: pl.BlockSpec((None, tm, w), lambda a, i: (a, i, 0))
    in_specs = [row(d), const((1, d)), const((d, att)), const((d, att)), const((d, LANES)),
                const((1, LANES)), const((att, LANES)), const((LANES, att)), const((1, att))]
    args = [x, g, wk, wv, wf, bfp, seg, seg_t, kgain]
    if prompt:
        assert pl.cdiv(t_out, tm) == t // tm
        kv_spec = pl.BlockSpec((None, att, tm), lambda a, i: (a, 0, i))
        kv_shape = jax.ShapeDtypeStruct((n, att, t_out), f32)
    else:
        kv_spec, kv_shape = row(att), jax.ShapeDtypeStruct((n, t, att), f32)
    out_specs = [kv_spec, kv_spec, row(LANES)]
    out_shape = [kv_shape, kv_shape, jax.ShapeDtypeStruct((n, t, LANES), f32)]
    scratch = []
    if prompt:
        ltri = jnp.asarray(np.tril(np.ones((tm, tm), np.float32)), bf16)
        in_specs += [const((tm, tm)), const((LANES, N_HEADS * LANES)),
                     const((1, N_HEADS * LANES)), const((1, N_HEADS * LANES))]
        args += [ltri, pneg, ones_row, kmask]
        out_specs += [row(LANES), pl.BlockSpec((None, N_HEADS, tm, LANES), lambda a, i: (a, 0, i, 0)),
                      pl.BlockSpec((None, tm // tk, att, tk), lambda a, i: (a, i, 0, 0))]
        out_shape += [jax.ShapeDtypeStruct((n, t, LANES), f32),
                      jax.ShapeDtypeStruct((n, N_HEADS, t, LANES), bf16),
                      jax.ShapeDtypeStruct((n, t // tk, att, tk), bf16)]
        scratch = [pltpu.VMEM((8, LANES), f32)]
    kern = functools.partial(_kv_kernel, tm=tm, tk=tk, prompt=prompt)
    return pl.pallas_call(
        kern,
        grid=(n, t // tm),
        in_specs=in_specs,
        out_specs=out_specs,
        out_shape=out_shape,
        scratch_shapes=scratch,
        compiler_params=_cparams(("arbitrary", "arbitrary")),
        name="kv_prompt" if prompt else "kv_sample",
    )(*args)


def _q_kernel(x_ref, g_ref, wq_ref, seg_ref, segt_ref, qg_ref, o_ref, *, transpose):
    xn = _rms(x_ref[...], g_ref[...])
    q = _head_rms(_dot(xn.astype(bf16), wq_ref[...]), seg_ref[...], segt_ref[...], qg_ref[...])
    if transpose:
        o_ref[...] = q.T.astype(bf16)
    else:
        o_ref[...] = q


def _q_proj(x, g, wq, qgain, layer, j, transpose):
    n, t, d = x.shape
    att = N_HEADS * HEAD_DIM
    tm = ROW_TILE if t % ROW_TILE == 0 else t
    seg, seg_t = _kv_constants()[:2]
    const = lambda shape: pl.BlockSpec(shape, lambda a, i: (0,) * len(shape))
    if transpose:
        out_spec = pl.BlockSpec((None, att, tm), lambda a, i: (a, 0, i))
        out_shape = jax.ShapeDtypeStruct((n, att, t), bf16)
    else:
        out_spec = pl.BlockSpec((None, tm, att), lambda a, i: (a, i, 0))
        out_shape = jax.ShapeDtypeStruct((n, t, att), f32)
    return pl.pallas_call(
        functools.partial(_q_kernel, transpose=transpose),
        grid=(n, t // tm),
        in_specs=[
            pl.BlockSpec((None, tm, d), lambda a, i: (a, i, 0)),
            pl.BlockSpec((None, 1, d), lambda a, i: (layer, 0, 0)),
            pl.BlockSpec((None, d, att), lambda a, i: (j, 0, 0)),
            const((att, LANES)), const((LANES, att)),
            pl.BlockSpec((None, 1, att), lambda a, i: (j, 0, 0)),
        ],
        out_specs=out_spec,
        out_shape=out_shape,
        compiler_params=_cparams(("arbitrary", "arbitrary")),
        name="q_proj",
    )(x, g, wq, seg, seg_t, qgain)


def _attn_prompt_kernel(qt_ref, ct_ref, kp_ref, vt_ref, o_ref, rhs_sc, s_sc, m_sc, acc_sc, *, t, hg):
    i = pl.program_id(2)
    row = lax.broadcasted_iota(jnp.int32, (HEAD_DIM, t), 0)
    for e in range(hg):
        hi, mid, lo = _split3(ct_ref[e:e + 1, :])
        aug = jnp.where(row == 0, hi.astype(f32),
                        jnp.where(row == 1, mid.astype(f32),
                                  jnp.where(row == 2, lo.astype(f32),
                                            jnp.where(row < N_AUG, 1.0, 0.0)))).astype(bf16)
        qe = qt_ref[e * HEAD_DIM:(e + 1) * HEAD_DIM, :]
        rhs_sc[e] = jnp.concatenate([qe, aug] if e % 2 == 0 else [aug, qe], axis=0)
        m_sc[e] = jnp.full((1, t), NEG_INF, f32)
        acc_sc[e] = jnp.zeros((HEAD_DIM + 16, t), f32)
    ones_blk = (lax.broadcasted_iota(jnp.int32, (16, t), 0) == 0).astype(bf16)
    causal = lax.broadcasted_iota(jnp.int32, (t, t), 0) <= lax.broadcasted_iota(jnp.int32, (t, t), 1)

    def scores(j, slot):
        off = pl.multiple_of(j * t, t)
        for e in range(hg):
            kb = kp_ref[e, pl.ds(off, t), :]
            s_sc[slot, e] = _dot(kb, rhs_sc[e])

    def update(j, slot, masked):
        ps, alphas = [], []
        for e in range(hg):
            s = s_sc[slot, e]
            if masked:
                s = jnp.where(causal, s, NEG_INF)
            m = m_sc[e]
            m_new = jnp.maximum(m, jnp.max(s, axis=0, keepdims=True))
            ps.append(jnp.exp2(s - m_new).astype(bf16))
            alphas.append(jnp.exp2(m - m_new))
            m_sc[e] = m_new
        for e in range(hg):
            lhs = jnp.concatenate([vt_ref[j, e * HEAD_DIM:(e + 1) * HEAD_DIM, :], ones_blk], axis=0)
            acc_sc[e] = alphas[e] * acc_sc[e] + _dot(lhs, ps[e])

    scores(0, 0)

    def pair(u, c):
        j = 2 * u
        scores(j + 1, 1)
        update(j, 0, False)
        scores(j + 2, 0)
        update(j + 1, 1, False)
        return c

    lax.fori_loop(0, lax.shift_right_logical(i, 1), pair, 0)

    @pl.when(i % 2 == 0)
    def _():
        update(i, 0, True)

    @pl.when(i % 2 == 1)
    def _():
        scores(i, 1)
        update(i - 1, 0, False)
        update(i, 1, True)

    outs = []
    for e in range(hg):
        acc = acc_sc[e]
        outs.append(acc[:HEAD_DIM] * pl.reciprocal(acc[HEAD_DIM:HEAD_DIM + 1], approx=False))
    o_ref[...] = jnp.concatenate(outs, axis=0).T.astype(bf16)


def _attn_prompt(qt, ct, kp, vt):
    n, att, tlen = qt.shape
    t, hg = ATT_TILE, HEAD_GROUP
    kern = functools.partial(_attn_prompt_kernel, t=t, hg=hg)
    return pl.pallas_call(
        kern,
        grid=(n, N_HEADS // hg, tlen // t),
        in_specs=[
            pl.BlockSpec((None, hg * HEAD_DIM, t), lambda a, g, i: (a, g, i)),
            pl.BlockSpec((None, None, hg, t), lambda a, g, i: (a, g, 0, i)),
            pl.BlockSpec((None, hg, tlen, LANES), lambda a, g, i: (a, g, 0, 0)),
            pl.BlockSpec((None, tlen // t, hg * HEAD_DIM, t), lambda a, g, i: (a, 0, g, 0)),
        ],
        out_specs=pl.BlockSpec((None, t, hg * HEAD_DIM), lambda a, g, i: (a, i, g)),
        out_shape=jax.ShapeDtypeStruct((n, tlen, att), bf16),
        scratch_shapes=[pltpu.VMEM((hg, LANES, t), bf16), pltpu.VMEM((2, hg, t, t), f32),
                        pltpu.VMEM((hg, 1, t), f32), pltpu.VMEM((hg, HEAD_DIM + 16, t), f32)],
        compiler_params=_cparams(("arbitrary", "arbitrary", "arbitrary")),
        name="attn_prompt",
    )(qt, ct, kp, vt)


def _attn_sample_kernel(*refs, n_steps, pps, page, seq_len):
    pt_ref, q_ref, knew_ref, vnew_ref, lfnew_ref = refs[:5]
    ck_refs = refs[5:5 + pps]
    cv_refs = refs[5 + pps:5 + 2 * pps]
    clf_refs = refs[5 + 2 * pps:5 + 3 * pps]
    usuf_ref, uinc_ref, o_ref, qbd_sc, m_sc, l_sc, acc_sc, carry_sc, e_sc, kpage, vpage = refs[5 + 3 * pps:]
    b = pl.program_id(0)
    j = pl.program_id(1)
    nq = seq_len * N_HEADS
    att = N_HEADS * HEAD_DIM

    def tile_t(x):
        return jnp.concatenate([x] * seq_len, axis=0)

    def softmax_update(s, pv_fn):
        m_prev = m_sc[...]
        m_new = jnp.maximum(m_prev, jnp.max(s, axis=1, keepdims=True))
        p = jnp.exp2(s - m_new)
        alpha = jnp.exp2(m_prev - m_new)
        l_sc[...] = alpha * l_sc[...] + jnp.sum(p, axis=1, keepdims=True)
        acc_sc[...] = alpha * acc_sc[...] + pv_fn(p.astype(bf16))
        m_sc[...] = m_new

    @pl.when(jnp.logical_and(b == 0, j == 0))
    def _():
        kpage[...] = jnp.zeros(kpage.shape, f32)
        vpage[...] = jnp.zeros(vpage.shape, f32)

    @pl.when(j == 0)
    def _():
        q = q_ref[...]
        own = (lax.broadcasted_iota(jnp.int32, (N_HEADS, att), 1) // HEAD_DIM
               == lax.broadcasted_iota(jnp.int32, (N_HEADS, att), 0))
        qbd_sc[...] = jnp.concatenate(
            [jnp.where(own, jnp.broadcast_to(q[t:t + 1], (N_HEADS, att)), 0.0) for t in range(seq_len)],
            axis=0).astype(bf16)
        e_t = _dot3(lfnew_ref[...], uinc_ref[...]) * LOG2E
        e_col = jnp.concatenate([e_t[:, t:t + 1] for t in range(seq_len)], axis=0)
        e_sc[...] = e_col
        carry_sc[...] = jnp.zeros(carry_sc.shape, f32)
        m_sc[...] = jnp.full(m_sc.shape, NEG_INF, f32)
        l_sc[...] = jnp.zeros(l_sc.shape, f32)
        acc_sc[...] = jnp.zeros(acc_sc.shape, f32)
        kpage[0:seq_len, :] = knew_ref[...]
        vpage[0:seq_len, :] = vnew_ref[...]
        s = _dot_nt(qbd_sc[...], kpage[...].astype(bf16))
        lane = lax.broadcasted_iota(jnp.int32, (nq, page), 1)
        t_row = lax.broadcasted_iota(jnp.int32, (nq, page), 0) // N_HEADS
        valid = jnp.logical_and(lane <= t_row, lane < seq_len)
        s = jnp.where(valid, s + (e_col - tile_t(e_t)), NEG_INF)
        softmax_update(s, lambda p: _dot(p, vpage[...].astype(bf16)))

    d_pages = []
    for r in range(pps):
        dd = _dot3(clf_refs[r][...], usuf_ref[...])
        d_pages.append((dd[:, :page] + carry_sc[...]) * LOG2E)
        carry_sc[...] = carry_sc[...] + dd[:, page:]
    kt = jnp.concatenate([ck_refs[r][...].reshape(att, page) for r in range(pps)], axis=1).astype(bf16)
    vt = jnp.concatenate([cv_refs[r][...].reshape(att, page) for r in range(pps)], axis=1).astype(bf16)
    s = _dot(qbd_sc[...], kt) + tile_t(jnp.concatenate(d_pages, axis=1)) + e_sc[...]
    softmax_update(s, lambda p: _dot_nt(p, vt))

    @pl.when(j == n_steps - 1)
    def _():
        o = acc_sc[...] * pl.reciprocal(l_sc[...], approx=False)
        own = (lax.broadcasted_iota(jnp.int32, (nq, att), 1) // HEAD_DIM
               == lax.broadcasted_iota(jnp.int32, (nq, att), 0) % N_HEADS)
        o = jnp.where(own, o, 0.0)
        o_ref[...] = jnp.concatenate(
            [jnp.sum(o[t * N_HEADS:(t + 1) * N_HEADS], axis=0, keepdims=True) for t in range(seq_len)], axis=0)


def _attn_sample(q, k_new, v_new, lf_new_t, cache_kt, cache_vt, cache_lf_t, page_table, seq_len):
    nb = q.shape[0]
    att = N_HEADS * HEAD_DIM
    n_pages = page_table.shape[1]
    page = cache_lf_t.shape[2]
    pps = SAMPLE_PAGES if n_pages % SAMPLE_PAGES == 0 else 1
    n_steps = n_pages // pps
    assert page == LANES
    usuf = np.concatenate([np.triu(np.ones((page, page), np.float32), 1).T, np.ones((page, page), np.float32)], axis=1)
    uinc = np.triu(np.ones((page, page), np.float32))
    pt = page_table.reshape(-1).astype(jnp.int32)
    seq = lambda shape: pl.BlockSpec((None,) + shape, lambda b, j, pt: (b,) + (0,) * len(shape))

    def pool(shape, r):
        return pl.BlockSpec((None,) + shape,
                            lambda b, j, pt: (pt[b * n_pages + n_pages - 1 - (j * pps + r)],) + (0,) * len(shape))

    const = lambda shape: pl.BlockSpec(shape, lambda b, j, pt: (0, 0))
    nq = seq_len * N_HEADS
    kern = functools.partial(_attn_sample_kernel, n_steps=n_steps, pps=pps, page=page, seq_len=seq_len)
    in_specs = [seq((seq_len, att)), seq((seq_len, att)), seq((seq_len, att)), seq((N_HEADS, page))]
    in_specs += [pool((N_HEADS, HEAD_DIM, page), r) for r in range(pps)]
    in_specs += [pool((N_HEADS, HEAD_DIM, page), r) for r in range(pps)]
    in_specs += [pool((N_HEADS, page), r) for r in range(pps)]
    in_specs += [const((page, 2 * page)), const((page, page))]
    return pl.pallas_call(
        kern,
        grid_spec=pltpu.PrefetchScalarGridSpec(
            num_scalar_prefetch=1,
            grid=(nb, n_steps),
            in_specs=in_specs,
            out_specs=seq((seq_len, att)),
            scratch_shapes=[pltpu.VMEM((nq, att), bf16),
                            pltpu.VMEM((nq, 1), f32), pltpu.VMEM((nq, 1), f32), pltpu.VMEM((nq, att), f32),
                            pltpu.VMEM((N_HEADS, page), f32), pltpu.VMEM((nq, 1), f32),
                            pltpu.VMEM((page, att), f32), pltpu.VMEM((page, att), f32)],
        ),
        out_shape=jax.ShapeDtypeStruct((nb, seq_len, att), f32),
        compiler_params=_cparams(("arbitrary", "arbitrary")),
        name="attn_sample",
    )(pt, q, k_new, v_new, lf_new_t, *([cache_kt] * pps), *([cache_vt] * pps), *([cache_lf_t] * pps),
      jnp.asarray(usuf, bf16), jnp.asarray(uinc, bf16))


def kernel(x_prompt, x_sample, cache_k, cache_v, cache_logf, state_conv, page_table, meta_tokens, norm_mix, norm_mlp, conv_w_in, conv_w, conv_w_out, norm_kv, w_kv, k_norm, w_f, b_f, w_q, q_norm, w_o, mlp_w1, mlp_w2):
    nb, seq, d = x_prompt.shape
    db, dseq, _ = x_sample.shape
    att = N_HEADS * HEAD_DIM
    n_a = conv_w_in.shape[0]
    depth = norm_mix.shape[0]
    t_real = seq + N_META
    pad_to = ROW_TILE * ATT_TILE // np.gcd(ROW_TILE, ATT_TILE)
    tp = -(-t_real // pad_to) * pad_to

    g_mix = norm_mix.reshape(depth, 1, d)
    g_mlp = norm_mlp.reshape(depth, 1, d)
    w_in = conv_w_in.astype(bf16)
    w_out = conv_w_out.astype(bf16)
    w1 = mlp_w1.astype(bf16)
    w2 = mlp_w2.astype(bf16)
    wk = w_kv[:, :att].astype(bf16)
    wv = w_kv[:, att:].astype(bf16)
    wv_t = w_kv[:, att:].T.astype(bf16)
    wf = jnp.pad(jnp.tile(w_f, (1, 3)), ((0, 0), (0, LANES - 3 * N_HEADS))).astype(bf16)
    bfp = jnp.pad(jnp.tile(b_f, 3), (0, LANES - 3 * N_HEADS)).reshape(1, LANES)
    g_kv = norm_kv.reshape(1, d)
    kgain = jnp.tile(k_norm, N_HEADS).reshape(1, att)
    wq = w_q.astype(bf16)
    qgain = (jnp.tile(q_norm, (1, N_HEADS)) * (HEAD_DIM ** -0.5 * LOG2E)).reshape(-1, 1, att)
    wo = w_o.astype(bf16)

    meta = jnp.broadcast_to(meta_tokens[None], (nb, N_META, d))
    xp = jnp.concatenate([meta, x_prompt, jnp.zeros((nb, tp - t_real, d), f32)], axis=1)
    conv_rows = []
    for layer in range(n_a):
        xp, st = _conv_prompt(xp, g_mix, w_in, conv_w, w_out, layer, t_real)
        conv_rows.append(st)
        xp = _mlp(xp.reshape(nb * tp, d), g_mlp, w1, w2, layer).reshape(nb, tp, d)
    k_p, v_p, lf_p, c_p, kp, vt = _kv_side(xp, g_kv, wk, wv_t, wf, bfp, kgain, prompt=True, t_out=t_real)
    ct = jnp.swapaxes(c_p[:, :, :N_HEADS], 1, 2).reshape(nb, N_HEADS // HEAD_GROUP, HEAD_GROUP, tp)
    for j in range(depth - n_a):
        layer = n_a + j
        qt = _q_proj(xp, g_mix, wq, qgain, layer, j, transpose=True)
        o = _attn_prompt(qt, ct, kp, vt)
        xp = _mlp(xp.reshape(nb * tp, d), g_mlp, w1, w2, layer,
                  proj=(o.reshape(nb * tp, att), wo, j)).reshape(nb, tp, d)
    y_prompt = xp[:, N_META:t_real]
    k_prompt = jnp.transpose(k_p.reshape(nb, N_HEADS, HEAD_DIM, t_real), (0, 3, 1, 2))
    v_prompt = jnp.transpose(v_p.reshape(nb, N_HEADS, HEAD_DIM, t_real), (0, 3, 1, 2))
    logf_prompt = lf_p[:, :t_real, :N_HEADS]
    conv_prompt = jnp.stack(conv_rows)

    m_s = db * dseq
    xs = x_sample.reshape(m_s, d)
    conv_rows = []
    for layer in range(n_a):
        xs, st = _conv_sample(xs, state_conv[layer], g_mix, w_in, conv_w, w_out, layer, dseq)
        conv_rows.append(st)
        xs = _mlp(xs, g_mlp, w1, w2, layer)
    k_s, v_s, lf_s = _kv_side(xs.reshape(1, m_s, d), g_kv, wk, wv, wf, bfp, kgain, prompt=False)
    page = cache_k.shape[1]
    cache_kt = jnp.transpose(cache_k, (0, 2, 3, 1))
    cache_vt = jnp.transpose(cache_v, (0, 2, 3, 1))
    cache_lf_t = jnp.swapaxes(cache_logf, 1, 2)
    k_new = k_s.reshape(db, dseq, att)
    v_new = v_s.reshape(db, dseq, att)
    lf_new = lf_s.reshape(db, dseq, LANES)[:, :, :N_HEADS]
    lf_new_t = jnp.pad(jnp.swapaxes(lf_new, 1, 2), ((0, 0), (0, 0), (0, page - dseq)))
    for j in range(depth - n_a):
        layer = n_a + j
        q = _q_proj(xs.reshape(1, m_s, d), g_mix, wq, qgain, layer, j, transpose=False)
        o = _attn_sample(q.reshape(db, dseq, att), k_new, v_new, lf_new_t, cache_kt, cache_vt, cache_lf_t,
                         page_table, dseq)
        o = o.reshape(m_s, att).astype(bf16)
        xs = _mlp(xs, g_mlp, w1, w2, layer, proj=(o, wo, j))
    y_sample = xs.reshape(db, dseq, d)
    k_sample = k_s.reshape(db, dseq, N_HEADS, HEAD_DIM)
    v_sample = v_s.reshape(db, dseq, N_HEADS, HEAD_DIM)
    logf_sample = lf_s.reshape(db, dseq, LANES)[:, :, :N_HEADS]
    conv_sample = jnp.stack(conv_rows)

    return (y_prompt, y_sample, k_prompt, v_prompt, logf_prompt, conv_prompt,
            k_sample, v_sample, logf_sample, conv_sample)
```

```python
import functools

import numpy as np
import jax
import jax.numpy as jnp
from jax import lax
from jax.experimental import pallas as pl
from jax.experimental.pallas import tpu as pltpu

N_HEADS = 16
HEAD_DIM = 64
N_META = 16
CONV_WIDTH = 3
RMS_EPS = 1e-6
NEG_INF = -1e30
LOG2E = 1.4426950408889634

LANES = 128
VMEM_LIMIT = 56 * 1024 * 1024
ROW_TILE = 384
MLP_ROW_TILE = 2 * ROW_TILE
ATT_TILE = 256
KV_ROW_TILE = ATT_TILE
HEAD_GROUP = 4
SAMPLE_PAGES = 16
N_AUG = 6

f32 = jnp.float32
bf16 = jnp.bfloat16


def _cparams(sem):
    return pltpu.CompilerParams(dimension_semantics=sem, vmem_limit_bytes=VMEM_LIMIT)


def _dot(a, b):
    return jnp.dot(a, b, preferred_element_type=f32)


def _dot_nt(a, b):
    return lax.dot_general(a, b, (((1,), (1,)), ((), ())), preferred_element_type=f32)


def _rms(x, g):
    ms = jnp.mean(x * x, axis=-1, keepdims=True)
    return x * lax.rsqrt(ms + RMS_EPS) * g


def _split3(x):
    hi = x.astype(bf16)
    r1 = x - hi.astype(f32)
    mid = r1.astype(bf16)
    lo = (r1 - mid.astype(f32)).astype(bf16)
    return hi, mid, lo


def _dot3(x, w):
    hi, mid, lo = _split3(x)
    return _dot(hi, w) + _dot(mid, w) + _dot(lo, w)


def _dot3_left(w, x):
    hi, mid, lo = _split3(x)
    return _dot(w, hi) + _dot(w, mid) + _dot(w, lo)


def _head_rms(y, seg, seg_t, gain):
    ssq = _dot((y * y).astype(bf16), seg)
    inv = lax.rsqrt(ssq * (1.0 / HEAD_DIM) + RMS_EPS)
    hi = inv.astype(bf16)
    lo = (inv - hi.astype(f32)).astype(bf16)
    lane = lax.broadcasted_iota(jnp.int32, inv.shape, 1)
    inv_full = _dot(jnp.where(lane < N_HEADS, hi, lo), seg_t)
    return y * inv_full * gain


def _conv_kernel(*refs, tm, mode, st_tile, st_row, seq_len):
    if mode == "prompt":
        x_ref, g_ref, win_ref, cw_ref, wout_ref, o_ref, st_ref, ubuf = refs
    else:
        x_ref, s1_ref, s2_ref, g_ref, win_ref, cw_ref, wout_ref, o_ref, st_ref = refs
    x = x_ref[...]
    d = x.shape[1]
    xn = _rms(x, g_ref[...])
    bch = _dot(xn.astype(bf16), win_ref[...])
    b = bch[:, :d]
    u = bch[:, d:2 * d] * bch[:, 2 * d:]
    if mode == "prompt":
        i = pl.program_id(1)

        @pl.when(i == 0)
        def _():
            ubuf[0:8, :] = jnp.zeros((8, d), f32)

        @pl.when(i > 0)
        def _():
            ubuf[0:8, :] = ubuf[tm:tm + 8, :]

        ubuf[8:tm + 8, :] = u
        prev1 = ubuf[7:tm + 7, :]
        prev2 = ubuf[6:tm + 6, :]

        @pl.when(i == st_tile)
        def _():
            st_ref[...] = u[st_row:st_row + 8, :]
    else:
        t = lax.broadcasted_iota(jnp.int32, (tm, d), 0) % seq_len
        prev1 = jnp.where(t >= 1, pltpu.roll(u, 1, 0), s1_ref[...])
        prev2 = jnp.where(t >= 2, pltpu.roll(u, 2, 0), s2_ref[...])
        st_ref[...] = u
    cw = cw_ref[...]
    conv = cw[2:3] * u + cw[0:1] * prev2 + cw[1:2] * prev1
    y = _dot((b * conv).astype(bf16), wout_ref[...])
    o_ref[...] = x + y


def _conv_prompt(x, g, w_in, cw, w_out, layer, t_real):
    n, tp, d = x.shape
    tm = ROW_TILE
    assert tp % tm == 0 and (t_real - 2) % 8 == 6
    st_tile, st_off = divmod(t_real - 2, tm)
    st_row = st_off // 8 * 8
    kern = functools.partial(_conv_kernel, tm=tm, mode="prompt", st_tile=st_tile, st_row=st_row, seq_len=0)
    out, st = pl.pallas_call(
        kern,
        grid=(n, tp // tm),
        in_specs=[
            pl.BlockSpec((None, tm, d), lambda a, i: (a, i, 0)),
            pl.BlockSpec((None, 1, d), lambda a, i: (layer, 0, 0)),
            pl.BlockSpec((None, d, 3 * d), lambda a, i: (layer, 0, 0)),
            pl.BlockSpec((None, CONV_WIDTH, d), lambda a, i: (layer, 0, 0)),
            pl.BlockSpec((None, d, d), lambda a, i: (layer, 0, 0)),
        ],
        out_specs=[
            pl.BlockSpec((None, tm, d), lambda a, i: (a, i, 0)),
            pl.BlockSpec((None, 8, d), lambda a, i: (a, 0, 0)),
        ],
        out_shape=[jax.ShapeDtypeStruct((n, tp, d), f32), jax.ShapeDtypeStruct((n, 8, d), f32)],
        scratch_shapes=[pltpu.VMEM((tm + 8, d), f32)],
        compiler_params=_cparams(("arbitrary", "arbitrary")),
        name="conv_prompt",
    )(x, g, w_in, cw, w_out)
    return out, st[:, 6:8]


def _conv_sample(x, state, g, w_in, cw, w_out, layer, seq_len):
    m, d = x.shape
    nb = m // seq_len
    z = jnp.zeros((nb, seq_len - 1, d), f32)
    s1 = jnp.concatenate([state[:, 1:2], z], axis=1).reshape(m, d)
    s2 = jnp.concatenate([state[:, 0:1], state[:, 1:2], z[:, 1:]], axis=1).reshape(m, d)
    kern = functools.partial(_conv_kernel, tm=m, mode="sample", st_tile=0, st_row=0, seq_len=seq_len)
    full = lambda i: (0, 0)
    out, u = pl.pallas_call(
        kern,
        grid=(1,),
        in_specs=[
            pl.BlockSpec((m, d), full),
            pl.BlockSpec((m, d), full),
            pl.BlockSpec((m, d), full),
            pl.BlockSpec((None, 1, d), lambda i: (layer, 0, 0)),
            pl.BlockSpec((None, d, 3 * d), lambda i: (layer, 0, 0)),
            pl.BlockSpec((None, CONV_WIDTH, d), lambda i: (layer, 0, 0)),
            pl.BlockSpec((None, d, d), lambda i: (layer, 0, 0)),
        ],
        out_specs=[pl.BlockSpec((m, d), full), pl.BlockSpec((m, d), full)],
        out_shape=[jax.ShapeDtypeStruct((m, d), f32), jax.ShapeDtypeStruct((m, d), f32)],
        compiler_params=_cparams(("arbitrary",)),
        name="conv_sample",
    )(x, s1, s2, g, w_in, cw, w_out)
    return out, u.reshape(nb, seq_len, d)[:, seq_len - 2:]


def _mlp_kernel(*refs, has_proj, n_chunks):
    if has_proj:
        x_ref, a_ref, wa_ref, g_ref, w1_ref, w2_ref, o_ref = refs
    else:
        x_ref, g_ref, w1_ref, w2_ref, o_ref = refs
    x = x_ref[...]
    if has_proj:
        x = x + _dot(a_ref[...], wa_ref[...])
    xn = _rms(x, g_ref[...]).astype(bf16)
    ck = w1_ref.shape[1] // n_chunks
    y = x
    for c in range(n_chunks):
        h = _dot(xn, w1_ref[:, c * ck:(c + 1) * ck])
        h = jnp.square(jnp.maximum(h, 0.0)).astype(bf16)
        y = y + _dot(h, w2_ref[c * ck:(c + 1) * ck, :])
    o_ref[...] = y


def _mlp(x, g, w1, w2, layer, proj=None):
    m, d = x.shape
    dff = w1.shape[2]
    tm = MLP_ROW_TILE if m % MLP_ROW_TILE == 0 else m
    row = lambda i: (i, 0)
    in_specs = [pl.BlockSpec((tm, d), row)]
    args = [x]
    if proj is not None:
        a, wa, pl_idx = proj
        in_specs += [pl.BlockSpec((tm, a.shape[1]), row),
                     pl.BlockSpec((None, a.shape[1], d), lambda i: (pl_idx, 0, 0))]
        args += [a, wa]
    in_specs += [
        pl.BlockSpec((None, 1, d), lambda i: (layer, 0, 0)),
        pl.BlockSpec((None, d, dff), lambda i: (layer, 0, 0)),
        pl.BlockSpec((None, dff, d), lambda i: (layer, 0, 0)),
    ]
    args += [g, w1, w2]
    kern = functools.partial(_mlp_kernel, has_proj=proj is not None, n_chunks=4)
    return pl.pallas_call(
        kern,
        grid=(m // tm,),
        in_specs=in_specs,
        out_specs=pl.BlockSpec((tm, d), row),
        out_shape=jax.ShapeDtypeStruct((m, d), f32),
        compiler_params=_cparams(("arbitrary",)),
        name="mlp",
    )(*args)


def _aug_offset(h):
    return LANES * h + (1 - h % 2) * HEAD_DIM


def _kv_constants():
    att = N_HEADS * HEAD_DIM
    seg = np.zeros((att, LANES), np.float32)
    seg[np.arange(att), np.arange(att) // HEAD_DIM] = 1.0
    seg[np.arange(att), N_HEADS + np.arange(att) // HEAD_DIM] = 1.0
    pneg = np.zeros((LANES, N_HEADS * LANES), np.float32)
    ones_row = np.zeros((1, N_HEADS * LANES), np.float32)
    kmask = np.zeros((1, N_HEADS * LANES), np.float32)
    for h in range(N_HEADS):
        a = _aug_offset(h)
        ones_row[0, a:a + 3] = 1.0
        for part in range(3):
            pneg[part * N_HEADS + h, a + 3 + part] = -1.0
        k0 = LANES * h + (h % 2) * HEAD_DIM
        kmask[0, k0:k0 + HEAD_DIM] = 1.0
    return (jnp.asarray(seg, bf16), jnp.asarray(seg.T, bf16), jnp.asarray(pneg, bf16),
            jnp.asarray(ones_row), jnp.asarray(kmask))


def _kv_kernel(*refs, tm, tk, prompt):
    if prompt:
        (x_ref, g_ref, wk_ref, wv_ref, wf_ref, bf_ref, seg_ref, segt_ref, kg_ref,
         ltri_ref, pneg_ref, ones_ref, kmask_ref,
         k_ref, v_ref, lf_ref, c_ref, kp_ref, vt_ref, carry) = refs
    else:
        (x_ref, g_ref, wk_ref, wv_ref, wf_ref, bf_ref, seg_ref, segt_ref, kg_ref,
         k_ref, v_ref, lf_ref) = refs
    xn = _rms(x_ref[...], g_ref[...])
    xb = xn.astype(bf16)
    k = _head_rms(_dot(xb, wk_ref[...]), seg_ref[...], segt_ref[...], kg_ref[...])
    z = _dot(xb, wf_ref[...]) + bf_ref[...]
    lane = lax.broadcasted_iota(jnp.int32, z.shape, 1)
    logf = jnp.where(lane < 3 * N_HEADS, jax.nn.log_sigmoid(z), 0.0)
    lf_ref[...] = logf
    if not prompt:
        k_ref[...] = k
        v_ref[...] = _dot(xb, wv_ref[...])
        return
    k_ref[...] = k.T
    v_t = _dot_nt(wv_ref[...], xb)
    v_ref[...] = v_t

    @pl.when(pl.program_id(1) == 0)
    def _():
        carry[...] = jnp.zeros(carry.shape, f32)

    c = _dot3_left(ltri_ref[...], logf) + carry[0:1, :]
    carry[...] = jnp.broadcast_to(c[tm - 1:tm, :], carry.shape)
    c2 = c * LOG2E
    c_ref[...] = c2
    hi, mid, lo = _split3(c2)
    pieces = jnp.where(lane < N_HEADS, hi, jnp.where(lane < 2 * N_HEADS, mid, lo))
    aug = _dot(pieces, pneg_ref[...]) + ones_ref[...]
    pairs = []
    for p in range(N_HEADS // 2):
        blk = k[:, p * LANES:(p + 1) * LANES]
        pairs += [blk, blk]
    k_rep = jnp.concatenate(pairs, axis=1)
    kp = jnp.where(kmask_ref[...] > 0.5, k_rep, aug).astype(bf16)
    for h in range(N_HEADS):
        kp_ref[h] = kp[:, h * LANES:(h + 1) * LANES]
    vt = v_t.astype(bf16)
    for j in range(tm // tk):
        vt_ref[j] = vt[:, j * tk:(j + 1) * tk]


def _kv_side(x, g, wk, wv, wf, bfp, kgain, prompt, t_out=None):
    n, t, d = x.shape
    att = N_HEADS * HEAD_DIM
    tm = KV_ROW_TILE if prompt else t
    tk = ATT_TILE
    assert t % tm == 0
    seg, seg_t, pneg, ones_row, kmask = _kv_constants()
    const = lambda shape: pl.BlockSpec(shape, lambda a, i: (0,) * len(shape))
    row = lambda w: pl.BlockSpec((None, tm, w), lambda a, i: (a, i, 0))
    in_specs = [row(d), const((1, d)), const((d, att)), const((d, att)), const((d, LANES)),
                const((1, LANES)), const((att, LANES)), const((LANES, att)), const((1, att))]
    args = [x, g, wk, wv, wf, bfp, seg, seg_t, kgain]
    if prompt:
        assert pl.cdiv(t_out, tm) == t // tm
        kv_spec = pl.BlockSpec((None, att, tm), lambda a, i: (a, 0, i))
        kv_shape = jax.ShapeDtypeStruct((n, att, t_out), f32)
    else:
        kv_spec, kv_shape = row(att), jax.ShapeDtypeStruct((n, t, att), f32)
    out_specs = [kv_spec, kv_spec, row(LANES)]
    out_shape = [kv_shape, kv_shape, jax.ShapeDtypeStruct((n, t, LANES), f32)]
    scratch = []
    if prompt:
        ltri = jnp.asarray(np.tril(np.ones((tm, tm), np.float32)), bf16)
        in_specs += [const((tm, tm)), const((LANES, N_HEADS * LANES)),
                     const((1, N_HEADS * LANES)), const((1, N_HEADS * LANES))]
        args += [ltri, pneg, ones_row, kmask]
        out_specs += [row(LANES), pl.BlockSpec((None, N_HEADS, tm, LANES), lambda a, i: (a, 0, i, 0)),
                      pl.BlockSpec((None, tm // tk, att, tk), lambda a, i: (a, i, 0, 0))]
        out_shape += [jax.ShapeDtypeStruct((n, t, LANES), f32),
                      jax.ShapeDtypeStruct((n, N_HEADS, t, LANES), bf16),
                      jax.ShapeDtypeStruct((n, t // tk, att, tk), bf16)]
        scratch = [pltpu.VMEM((8, LANES), f32)]
    kern = functools.partial(_kv_kernel, tm=tm, tk=tk, prompt=prompt)
    return pl.pallas_call(
        kern,
        grid=(n, t // tm),
        in_specs=in_specs,
        out_specs=out_specs,
        out_shape=out_shape,
        scratch_shapes=scratch,
        compiler_params=_cparams(("arbitrary", "arbitrary")),
        name="kv_prompt" if prompt else "kv_sample",
    )(*args)


def _q_kernel(x_ref, g_ref, wq_ref, seg_ref, segt_ref, qg_ref, o_ref, *, transpose):
    xn = _rms(x_ref[...], g_ref[...])
    q = _head_rms(_dot(xn.astype(bf16), wq_ref[...]), seg_ref[...], segt_ref[...], qg_ref[...])
    if transpose:
        o_ref[...] = q.T.astype(bf16)
    else:
        o_ref[...] = q


def _q_proj(x, g, wq, qgain, layer, j, transpose):
    n, t, d = x.shape
    att = N_HEADS * HEAD_DIM
    tm = ROW_TILE if t % ROW_TILE == 0 else t
    seg, seg_t = _kv_constants()[:2]
    const = lambda shape: pl.BlockSpec(shape, lambda a, i: (0,) * len(shape))
    if transpose:
        out_spec = pl.BlockSpec((None, att, tm), lambda a, i: (a, 0, i))
        out_shape = jax.ShapeDtypeStruct((n, att, t), bf16)
    else:
        out_spec = pl.BlockSpec((None, tm, att), lambda a, i: (a, i, 0))
        out_shape = jax.ShapeDtypeStruct((n, t, att), f32)
    return pl.pallas_call(
        functools.partial(_q_kernel, transpose=transpose),
        grid=(n, t // tm),
        in_specs=[
            pl.BlockSpec((None, tm, d), lambda a, i: (a, i, 0)),
            pl.BlockSpec((None, 1, d), lambda a, i: (layer, 0, 0)),
            pl.BlockSpec((None, d, att), lambda a, i: (j, 0, 0)),
            const((att, LANES)), const((LANES, att)),
            pl.BlockSpec((None, 1, att), lambda a, i: (j, 0, 0)),
        ],
        out_specs=out_spec,
        out_shape=out_shape,
        compiler_params=_cparams(("arbitrary", "arbitrary")),
        name="q_proj",
    )(x, g, wq, seg, seg_t, qgain)


def _attn_prompt_kernel(qt_ref, ct_ref, kp_ref, vt_ref, o_ref, rhs_sc, s_sc, m_sc, acc_sc, *, t, hg):
    i = pl.program_id(2)
    row = lax.broadcasted_iota(jnp.int32, (HEAD_DIM, t), 0)
    for e in range(hg):
        hi, mid, lo = _split3(ct_ref[e:e + 1, :])
        aug = jnp.where(row == 0, hi.astype(f32),
                        jnp.where(row == 1, mid.astype(f32),
                                  jnp.where(row == 2, lo.astype(f32),
                                            jnp.where(row < N_AUG, 1.0, 0.0)))).astype(bf16)
        qe = qt_ref[e * HEAD_DIM:(e + 1) * HEAD_DIM, :]
        rhs_sc[e] = jnp.concatenate([qe, aug] if e % 2 == 0 else [aug, qe], axis=0)
        m_sc[e] = jnp.full((1, t), NEG_INF, f32)
        acc_sc[e] = jnp.zeros((HEAD_DIM + 16, t), f32)
    ones_blk = (lax.broadcasted_iota(jnp.int32, (16, t), 0) == 0).astype(bf16)
    causal = lax.broadcasted_iota(jnp.int32, (t, t), 0) <= lax.broadcasted_iota(jnp.int32, (t, t), 1)

    def scores(j, slot):
        off = pl.multiple_of(j * t, t)
        for e in range(hg):
            kb = kp_ref[e, pl.ds(off, t), :]
            s_sc[slot, e] = _dot(kb, rhs_sc[e])

    def update(j, slot, masked):
        ps, alphas = [], []
        for e in range(hg):
            s = s_sc[slot, e]
            if masked:
                s = jnp.where(causal, s, NEG_INF)
            m = m_sc[e]
            m_new = jnp.maximum(m, jnp.max(s, axis=0, keepdims=True))
            ps.append(jnp.exp2(s - m_new).astype(bf16))
            alphas.append(jnp.exp2(m - m_new))
            m_sc[e] = m_new
        for e in range(hg):
            lhs = jnp.concatenate([vt_ref[j, e * HEAD_DIM:(e + 1) * HEAD_DIM, :], ones_blk], axis=0)
            acc_sc[e] = alphas[e] * acc_sc[e] + _dot(lhs, ps[e])

    scores(0, 0)

    def pair(u, c):
        j = 2 * u
        scores(j + 1, 1)
        update(j, 0, False)
        scores(j + 2, 0)
        update(j + 1, 1, False)
        return c

    lax.fori_loop(0, lax.shift_right_logical(i, 1), pair, 0)

    @pl.when(i % 2 == 0)
    def _():
        update(i, 0, True)

    @pl.when(i % 2 == 1)
    def _():
        scores(i, 1)
        update(i - 1, 0, False)
        update(i, 1, True)

    outs = []
    for e in range(hg):
        acc = acc_sc[e]
        outs.append(acc[:HEAD_DIM] * pl.reciprocal(acc[HEAD_DIM:HEAD_DIM + 1], approx=False))
    o_ref[...] = jnp.concatenate(outs, axis=0).T.astype(bf16)


def _attn_prompt(qt, ct, kp, vt):
    n, att, tlen = qt.shape
    t, hg = ATT_TILE, HEAD_GROUP
    kern = functools.partial(_attn_prompt_kernel, t=t, hg=hg)
    return pl.pallas_call(
        kern,
        grid=(n, N_HEADS // hg, tlen // t),
        in_specs=[
            pl.BlockSpec((None, hg * HEAD_DIM, t), lambda a, g, i: (a, g, i)),
            pl.BlockSpec((None, None, hg, t), lambda a, g, i: (a, g, 0, i)),
            pl.BlockSpec((None, hg, tlen, LANES), lambda a, g, i: (a, g, 0, 0)),
            pl.BlockSpec((None, tlen // t, hg * HEAD_DIM, t), lambda a, g, i: (a, 0, g, 0)),
        ],
        out_specs=pl.BlockSpec((None, t, hg * HEAD_DIM), lambda a, g, i: (a, i, g)),
        out_shape=jax.ShapeDtypeStruct((n, tlen, att), bf16),
        scratch_shapes=[pltpu.VMEM((hg, LANES, t), bf16), pltpu.VMEM((2, hg, t, t), f32),
                        pltpu.VMEM((hg, 1, t), f32), pltpu.VMEM((hg, HEAD_DIM + 16, t), f32)],
        compiler_params=_cparams(("arbitrary", "arbitrary", "arbitrary")),
        name="attn_prompt",
    )(qt, ct, kp, vt)


def _attn_sample_kernel(*refs, n_steps, pps, page, seq_len):
    pt_ref, q_ref, knew_ref, vnew_ref, lfnew_ref = refs[:5]
    ck_refs = refs[5:5 + pps]
    cv_refs = refs[5 + pps:5 + 2 * pps]
    clf_refs = refs[5 + 2 * pps:5 + 3 * pps]
    usuf_ref, uinc_ref, o_ref, qbd_sc, m_sc, l_sc, acc_sc, carry_sc, e_sc, kpage, vpage = refs[5 + 3 * pps:]
    b = pl.program_id(0)
    j = pl.program_id(1)
    nq = seq_len * N_HEADS
    att = N_HEADS * HEAD_DIM

    def tile_t(x):
        return jnp.concatenate([x] * seq_len, axis=0)

    def softmax_update(s, pv_fn):
        m_prev = m_sc[...]
        m_new = jnp.maximum(m_prev, jnp.max(s, axis=1, keepdims=True))
        p = jnp.exp2(s - m_new)
        alpha = jnp.exp2(m_prev - m_new)
        l_sc[...] = alpha * l_sc[...] + jnp.sum(p, axis=1, keepdims=True)
        acc_sc[...] = alpha * acc_sc[...] + pv_fn(p.astype(bf16))
        m_sc[...] = m_new

    @pl.when(jnp.logical_and(b == 0, j == 0))
    def _():
        kpage[...] = jnp.zeros(kpage.shape, f32)
        vpage[...] = jnp.zeros(vpage.shape, f32)

    @pl.when(j == 0)
    def _():
        q = q_ref[...]
        own = (lax.broadcasted_iota(jnp.int32, (N_HEADS, att), 1) // HEAD_DIM
               == lax.broadcasted_iota(jnp.int32, (N_HEADS, att), 0))
        qbd_sc[...] = jnp.concatenate(
            [jnp.where(own, jnp.broadcast_to(q[t:t + 1], (N_HEADS, att)), 0.0) for t in range(seq_len)],
            axis=0).astype(bf16)
        e_t = _dot3(lfnew_ref[...], uinc_ref[...]) * LOG2E
        e_col = jnp.concatenate([e_t[:, t:t + 1] for t in range(seq_len)], axis=0)
        e_sc[...] = e_col
        carry_sc[...] = jnp.zeros(carry_sc.shape, f32)
        m_sc[...] = jnp.full(m_sc.shape, NEG_INF, f32)
        l_sc[...] = jnp.zeros(l_sc.shape, f32)
        acc_sc[...] = jnp.zeros(acc_sc.shape, f32)
        kpage[0:seq_len, :] = knew_ref[...]
        vpage[0:seq_len, :] = vnew_ref[...]
        s = _dot_nt(qbd_sc[...], kpage[...].astype(bf16))
        lane = lax.broadcasted_iota(jnp.int32, (nq, page), 1)
        t_row = lax.broadcasted_iota(jnp.int32, (nq, page), 0) // N_HEADS
        valid = jnp.logical_and(lane <= t_row, lane < seq_len)
        s = jnp.where(valid, s + (e_col - tile_t(e_t)), NEG_INF)
        softmax_update(s, lambda p: _dot(p, vpage[...].astype(bf16)))

    d_pages = []
    for r in range(pps):
        dd = _dot3(clf_refs[r][...], usuf_ref[...])
        d_pages.append((dd[:, :page] + carry_sc[...]) * LOG2E)
        carry_sc[...] = carry_sc[...] + dd[:, page:]
    kt = jnp.concatenate([ck_refs[r][...].astype(bf16).reshape(att, page) for r in range(pps)], axis=1)
    vt = jnp.concatenate([cv_refs[r][...].astype(bf16).reshape(att, page) for r in range(pps)], axis=1)
    s = _dot(qbd_sc[...], kt) + tile_t(jnp.concatenate(d_pages, axis=1)) + e_sc[...]
    softmax_update(s, lambda p: _dot_nt(p, vt))

    @pl.when(j == n_steps - 1)
    def _():
        o = acc_sc[...] * pl.reciprocal(l_sc[...], approx=False)
        own = (lax.broadcasted_iota(jnp.int32, (nq, att), 1) // HEAD_DIM
               == lax.broadcasted_iota(jnp.int32, (nq, att), 0) % N_HEADS)
        o = jnp.where(own, o, 0.0)
        o_ref[...] = jnp.concatenate(
            [jnp.sum(o[t * N_HEADS:(t + 1) * N_HEADS], axis=0, keepdims=True) for t in range(seq_len)], axis=0)


def _attn_sample(q, k_new, v_new, lf_new_t, cache_kt, cache_vt, cache_lf_t, page_table, seq_len):
    nb = q.shape[0]
    att = N_HEADS * HEAD_DIM
    n_pages = page_table.shape[1]
    page = cache_lf_t.shape[2]
    pps = SAMPLE_PAGES if n_pages % SAMPLE_PAGES == 0 else 1
    n_steps = n_pages // pps
    assert page == LANES
    usuf = np.concatenate([np.triu(np.ones((page, page), np.float32), 1).T, np.ones((page, page), np.float32)], axis=1)
    uinc = np.triu(np.ones((page, page), np.float32))
    pt = page_table.reshape(-1).astype(jnp.int32)
    seq = lambda shape: pl.BlockSpec((None,) + shape, lambda b, j, pt: (b,) + (0,) * len(shape))

    def pool(shape, r):
        return pl.BlockSpec((None,) + shape,
                            lambda b, j, pt: (pt[b * n_pages + n_pages - 1 - (j * pps + r)],) + (0,) * len(shape))

    const = lambda shape: pl.BlockSpec(shape, lambda b, j, pt: (0, 0))
    nq = seq_len * N_HEADS
    kern = functools.partial(_attn_sample_kernel, n_steps=n_steps, pps=pps, page=page, seq_len=seq_len)
    in_specs = [seq((seq_len, att)), seq((seq_len, att)), seq((seq_len, att)), seq((N_HEADS, page))]
    in_specs += [pool((N_HEADS, HEAD_DIM, page), r) for r in range(pps)]
    in_specs += [pool((N_HEADS, HEAD_DIM, page), r) for r in range(pps)]
    in_specs += [pool((N_HEADS, page), r) for r in range(pps)]
    in_specs += [const((page, 2 * page)), const((page, page))]
    return pl.pallas_call(
        kern,
        grid_spec=pltpu.PrefetchScalarGridSpec(
            num_scalar_prefetch=1,
            grid=(nb, n_steps),
            in_specs=in_specs,
            out_specs=seq((seq_len, att)),
            scratch_shapes=[pltpu.VMEM((nq, att), bf16),
                            pltpu.VMEM((nq, 1), f32), pltpu.VMEM((nq, 1), f32), pltpu.VMEM((nq, att), f32),
                            pltpu.VMEM((N_HEADS, page), f32), pltpu.VMEM((nq, 1), f32),
                            pltpu.VMEM((page, att), f32), pltpu.VMEM((page, att), f32)],
        ),
        out_shape=jax.ShapeDtypeStruct((nb, seq_len, att), f32),
        compiler_params=_cparams(("arbitrary", "arbitrary")),
        name="attn_sample",
    )(pt, q, k_new, v_new, lf_new_t, *([cache_kt] * pps), *([cache_vt] * pps), *([cache_lf_t] * pps),
      jnp.asarray(usuf, bf16), jnp.asarray(uinc, bf16))


def kernel(x_prompt, x_sample, cache_k, cache_v, cache_logf, state_conv, page_table, meta_tokens, norm_mix, norm_mlp, conv_w_in, conv_w, conv_w_out, norm_kv, w_kv, k_norm, w_f, b_f, w_q, q_norm, w_o, mlp_w1, mlp_w2):
    nb, seq, d = x_prompt.shape
    db, dseq, _ = x_sample.shape
    att = N_HEADS * HEAD_DIM
    n_a = conv_w_in.shape[0]
    depth = norm_mix.shape[0]
    t_real = seq + N_META
    pad_to = ROW_TILE * ATT_TILE // np.gcd(ROW_TILE, ATT_TILE)
    tp = -(-t_real // pad_to) * pad_to

    g_mix = norm_mix.reshape(depth, 1, d)
    g_mlp = norm_mlp.reshape(depth, 1, d)
    w_in = conv_w_in.astype(bf16)
    w_out = conv_w_out.astype(bf16)
    w1 = mlp_w1.astype(bf16)
    w2 = mlp_w2.astype(bf16)
    wk = w_kv[:, :att].astype(bf16)
    wv = w_kv[:, att:].astype(bf16)
    wv_t = w_kv[:, att:].T.astype(bf16)
    wf = jnp.pad(jnp.tile(w_f, (1, 3)), ((0, 0), (0, LANES - 3 * N_HEADS))).astype(bf16)
    bfp = jnp.pad(jnp.tile(b_f, 3), (0, LANES - 3 * N_HEADS)).reshape(1, LANES)
    g_kv = norm_kv.reshape(1, d)
    kgain = jnp.tile(k_norm, N_HEADS).reshape(1, att)
    wq = w_q.astype(bf16)
    qgain = (jnp.tile(q_norm, (1, N_HEADS)) * (HEAD_DIM ** -0.5 * LOG2E)).reshape(-1, 1, att)
    wo = w_o.astype(bf16)

    meta = jnp.broadcast_to(meta_tokens[None], (nb, N_META, d))
    xp = jnp.concatenate([meta, x_prompt, jnp.zeros((nb, tp - t_real, d), f32)], axis=1)
    conv_rows = []
    for layer in range(n_a):
        xp, st = _conv_prompt(xp, g_mix, w_in, conv_w, w_out, layer, t_real)
        conv_rows.append(st)
        xp = _mlp(xp.reshape(nb * tp, d), g_mlp, w1, w2, layer).reshape(nb, tp, d)
    k_p, v_p, lf_p, c_p, kp, vt = _kv_side(xp, g_kv, wk, wv_t, wf, bfp, kgain, prompt=True, t_out=t_real)
    ct = jnp.swapaxes(c_p[:, :, :N_HEADS], 1, 2).reshape(nb, N_HEADS // HEAD_GROUP, HEAD_GROUP, tp)
    for j in range(depth - n_a):
        layer = n_a + j
        qt = _q_proj(xp, g_mix, wq, qgain, layer, j, transpose=True)
        o = _attn_prompt(qt, ct, kp, vt)
        xp = _mlp(xp.reshape(nb * tp, d), g_mlp, w1, w2, layer,
                  proj=(o.reshape(nb * tp, att), wo, j)).reshape(nb, tp, d)
    y_prompt = xp[:, N_META:t_real]
    k_prompt = jnp.transpose(k_p.reshape(nb, N_HEADS, HEAD_DIM, t_real), (0, 3, 1, 2))
    v_prompt = jnp.transpose(v_p.reshape(nb, N_HEADS, HEAD_DIM, t_real), (0, 3, 1, 2))
    logf_prompt = lf_p[:, :t_real, :N_HEADS]
    conv_prompt = jnp.stack(conv_rows)

    m_s = db * dseq
    xs = x_sample.reshape(m_s, d)
    conv_rows = []
    for layer in range(n_a):
        xs, st = _conv_sample(xs, state_conv[layer], g_mix, w_in, conv_w, w_out, layer, dseq)
        conv_rows.append(st)
        xs = _mlp(xs, g_mlp, w1, w2, layer)
    k_s, v_s, lf_s = _kv_side(xs.reshape(1, m_s, d), g_kv, wk, wv, wf, bfp, kgain, prompt=False)
    page = cache_k.shape[1]
    cache_kt = jnp.transpose(cache_k, (0, 2, 3, 1))
    cache_vt = jnp.transpose(cache_v, (0, 2, 3, 1))
    cache_lf_t = jnp.swapaxes(cache_logf, 1, 2)
    k_new = k_s.reshape(db, dseq, att)
    v_new = v_s.reshape(db, dseq, att)
    lf_new = lf_s.reshape(db, dseq, LANES)[:, :, :N_HEADS]
    lf_new_t = jnp.pad(jnp.swapaxes(lf_new, 1, 2), ((0, 0), (0, 0), (0, page - dseq)))
    for j in range(depth - n_a):
        layer = n_a + j
        q = _q_proj(xs.reshape(1, m_s, d), g_mix, wq, qgain, layer, j, transpose=False)
        o = _attn_sample(q.reshape(db, dseq, att), k_new, v_new, lf_new_t, cache_kt, cache_vt, cache_lf_t,
                         page_table, dseq)
        o = o.reshape(m_s, att).astype(bf16)
        xs = _mlp(xs, g_mlp, w1, w2, layer, proj=(o, wo, j))
    y_sample = xs.reshape(db, dseq, d)
    k_sample = k_s.reshape(db, dseq, N_HEADS, HEAD_DIM)
    v_sample = v_s.reshape(db, dseq, N_HEADS, HEAD_DIM)
    logf_sample = lf_s.reshape(db, dseq, LANES)[:, :, :N_HEADS]
    conv_sample = jnp.stack(conv_rows)

    return (y_prompt, y_sample, k_prompt, v_prompt, logf_prompt, conv_prompt,
            k_sample, v_sample, logf_sample, conv_sample)
```

```python
import functools

import numpy as np
import jax
import jax.numpy as jnp
from jax import lax
from jax.experimental import pallas as pl
from jax.experimental.pallas import tpu as pltpu

N_HEADS = 16
HEAD_DIM = 64
N_META = 16
CONV_WIDTH = 3
RMS_EPS = 1e-6
NEG_INF = -1e30
LOG2E = 1.4426950408889634

LANES = 128
VMEM_LIMIT = 56 * 1024 * 1024
ROW_TILE = 384
MLP_ROW_TILE = 2 * ROW_TILE
ATT_TILE = 256
KV_ROW_TILE = ATT_TILE
HEAD_GROUP = 4
SAMPLE_PAGES = 16
N_AUG = 6

f32 = jnp.float32
bf16 = jnp.bfloat16


def _cparams(sem):
    return pltpu.CompilerParams(dimension_semantics=sem, vmem_limit_bytes=VMEM_LIMIT)


def _dot(a, b):
    return jnp.dot(a, b, preferred_element_type=f32)


def _dot_nt(a, b):
    return lax.dot_general(a, b, (((1,), (1,)), ((), ())), preferred_element_type=f32)


def _rms(x, g):
    ms = jnp.mean(x * x, axis=-1, keepdims=True)
    return x * lax.rsqrt(ms + RMS_EPS) * g


def _split3(x):
    hi = x.astype(bf16)
    r1 = x - hi.astype(f32)
    mid = r1.astype(bf16)
    lo = (r1 - mid.astype(f32)).astype(bf16)
    return hi, mid, lo


def _dot3(x, w):
    hi, mid, lo = _split3(x)
    return _dot(hi, w) + _dot(mid, w) + _dot(lo, w)


def _dot3_left(w, x):
    hi, mid, lo = _split3(x)
    return _dot(w, hi) + _dot(w, mid) + _dot(w, lo)


def _head_rms(y, seg, seg_t, gain):
    ssq = _dot((y * y).astype(bf16), seg)
    inv = lax.rsqrt(ssq * (1.0 / HEAD_DIM) + RMS_EPS)
    hi = inv.astype(bf16)
    lo = (inv - hi.astype(f32)).astype(bf16)
    lane = lax.broadcasted_iota(jnp.int32, inv.shape, 1)
    inv_full = _dot(jnp.where(lane < N_HEADS, hi, lo), seg_t)
    return y * inv_full * gain


def _conv_kernel(*refs, tm, mode, st_tile, st_row, seq_len):
    if mode == "prompt":
        x_ref, g_ref, win_ref, cw_ref, wout_ref, o_ref, st_ref, ubuf = refs
    else:
        x_ref, s1_ref, s2_ref, g_ref, win_ref, cw_ref, wout_ref, o_ref, st_ref = refs
    x = x_ref[...]
    d = x.shape[1]
    xn = _rms(x, g_ref[...])
    bch = _dot(xn.astype(bf16), win_ref[...])
    b = bch[:, :d]
    u = bch[:, d:2 * d] * bch[:, 2 * d:]
    if mode == "prompt":
        i = pl.program_id(1)

        @pl.when(i == 0)
        def _():
            ubuf[0:8, :] = jnp.zeros((8, d), f32)

        @pl.when(i > 0)
        def _():
            ubuf[0:8, :] = ubuf[tm:tm + 8, :]

        ubuf[8:tm + 8, :] = u
        prev1 = ubuf[7:tm + 7, :]
        prev2 = ubuf[6:tm + 6, :]

        @pl.when(i == st_tile)
        def _():
            st_ref[...] = u[st_row:st_row + 8, :]
    else:
        t = lax.broadcasted_iota(jnp.int32, (tm, d), 0) % seq_len
        prev1 = jnp.where(t >= 1, pltpu.roll(u, 1, 0), s1_ref[...])
        prev2 = jnp.where(t >= 2, pltpu.roll(u, 2, 0), s2_ref[...])
        st_ref[...] = u
    cw = cw_ref[...]
    conv = cw[2:3] * u + cw[0:1] * prev2 + cw[1:2] * prev1
    y = _dot((b * conv).astype(bf16), wout_ref[...])
    o_ref[...] = x + y


def _conv_prompt(x, g, w_in, cw, w_out, layer, t_real):
    n, tp, d = x.shape
    tm = ROW_TILE
    assert tp % tm == 0 and (t_real - 2) % 8 == 6
    st_tile, st_off = divmod(t_real - 2, tm)
    st_row = st_off // 8 * 8
    kern = functools.partial(_conv_kernel, tm=tm, mode="prompt", st_tile=st_tile, st_row=st_row, seq_len=0)
    out, st = pl.pallas_call(
        kern,
        grid=(n, tp // tm),
        in_specs=[
            pl.BlockSpec((None, tm, d), lambda a, i: (a, i, 0)),
            pl.BlockSpec((None, 1, d), lambda a, i: (layer, 0, 0)),
            pl.BlockSpec((None, d, 3 * d), lambda a, i: (layer, 0, 0)),
            pl.BlockSpec((None, CONV_WIDTH, d), lambda a, i: (layer, 0, 0)),
            pl.BlockSpec((None, d, d), lambda a, i: (layer, 0, 0)),
        ],
        out_specs=[
            pl.BlockSpec((None, tm, d), lambda a, i: (a, i, 0)),
            pl.BlockSpec((None, 8, d), lambda a, i: (a, 0, 0)),
        ],
        out_shape=[jax.ShapeDtypeStruct((n, tp, d), f32), jax.ShapeDtypeStruct((n, 8, d), f32)],
        scratch_shapes=[pltpu.VMEM((tm + 8, d), f32)],
        compiler_params=_cparams(("arbitrary", "arbitrary")),
        name="conv_prompt",
    )(x, g, w_in, cw, w_out)
    return out, st[:, 6:8]


def _conv_sample(x, state, g, w_in, cw, w_out, layer, seq_len):
    m, d = x.shape
    nb = m // seq_len
    z = jnp.zeros((nb, seq_len - 1, d), f32)
    s1 = jnp.concatenate([state[:, 1:2], z], axis=1).reshape(m, d)
    s2 = jnp.concatenate([state[:, 0:1], state[:, 1:2], z[:, 1:]], axis=1).reshape(m, d)
    kern = functools.partial(_conv_kernel, tm=m, mode="sample", st_tile=0, st_row=0, seq_len=seq_len)
    full = lambda i: (0, 0)
    out, u = pl.pallas_call(
        kern,
        grid=(1,),
        in_specs=[
            pl.BlockSpec((m, d), full),
            pl.BlockSpec((m, d), full),
            pl.BlockSpec((m, d), full),
            pl.BlockSpec((None, 1, d), lambda i: (layer, 0, 0)),
            pl.BlockSpec((None, d, 3 * d), lambda i: (layer, 0, 0)),
            pl.BlockSpec((None, CONV_WIDTH, d), lambda i: (layer, 0, 0)),
            pl.BlockSpec((None, d, d), lambda i: (layer, 0, 0)),
        ],
        out_specs=[pl.BlockSpec((m, d), full), pl.BlockSpec((m, d), full)],
        out_shape=[jax.ShapeDtypeStruct((m, d), f32), jax.ShapeDtypeStruct((m, d), f32)],
        compiler_params=_cparams(("arbitrary",)),
        name="conv_sample",
    )(x, s1, s2, g, w_in, cw, w_out)
    return out, u.reshape(nb, seq_len, d)[:, seq_len - 2:]


def _mlp_kernel(*refs, has_proj, n_chunks):
    if has_proj:
        x_ref, a_ref, wa_ref, g_ref, w1_ref, w2_ref, o_ref = refs
    else:
        x_ref, g_ref, w1_ref, w2_ref, o_ref = refs
    x = x_ref[...]
    if has_proj:
        x = x + _dot(a_ref[...], wa_ref[...])
    xn = _rms(x, g_ref[...]).astype(bf16)
    ck = w1_ref.shape[1] // n_chunks
    y = x
    for c in range(n_chunks):
        h = _dot(xn, w1_ref[:, c * ck:(c + 1) * ck])
        h = jnp.square(jnp.maximum(h, 0.0)).astype(bf16)
        y = y + _dot(h, w2_ref[c * ck:(c + 1) * ck, :])
    o_ref[...] = y


def _mlp(x, g, w1, w2, layer, proj=None):
    m, d = x.shape
    dff = w1.shape[2]
    tm = MLP_ROW_TILE if m % MLP_ROW_TILE == 0 else m
    row = lambda i: (i, 0)
    in_specs = [pl.BlockSpec((tm, d), row)]
    args = [x]
    if proj is not None:
        a, wa, pl_idx = proj
        in_specs += [pl.BlockSpec((tm, a.shape[1]), row),
                     pl.BlockSpec((None, a.shape[1], d), lambda i: (pl_idx, 0, 0))]
        args += [a, wa]
    in_specs += [
        pl.BlockSpec((None, 1, d), lambda i: (layer, 0, 0)),
        pl.BlockSpec((None, d, dff), lambda i: (layer, 0, 0)),
        pl.BlockSpec((None, dff, d), lambda i: (layer, 0, 0)),
    ]
    args += [g, w1, w2]
    kern = functools.partial(_mlp_kernel, has_proj=proj is not None, n_chunks=4)
    return pl.pallas_call(
        kern,
        grid=(m // tm,),
        in_specs=in_specs,
        out_specs=pl.BlockSpec((tm, d), row),
        out_shape=jax.ShapeDtypeStruct((m, d), f32),
        compiler_params=_cparams(("arbitrary",)),
        name="mlp",
    )(*args)


def _aug_offset(h):
    return LANES * h + (1 - h % 2) * HEAD_DIM


def _kv_constants():
    att = N_HEADS * HEAD_DIM
    seg = np.zeros((att, LANES), np.float32)
    seg[np.arange(att), np.arange(att) // HEAD_DIM] = 1.0
    seg[np.arange(att), N_HEADS + np.arange(att) // HEAD_DIM] = 1.0
    pneg = np.zeros((LANES, N_HEADS * LANES), np.float32)
    ones_row = np.zeros((1, N_HEADS * LANES), np.float32)
    kmask = np.zeros((1, N_HEADS * LANES), np.float32)
    for h in range(N_HEADS):
        a = _aug_offset(h)
        ones_row[0, a:a + 3] = 1.0
        for part in range(3):
            pneg[part * N_HEADS + h, a + 3 + part] = -1.0
        k0 = LANES * h + (h % 2) * HEAD_DIM
        kmask[0, k0:k0 + HEAD_DIM] = 1.0
    return (jnp.asarray(seg, bf16), jnp.asarray(seg.T, bf16), jnp.asarray(pneg, bf16),
            jnp.asarray(ones_row), jnp.asarray(kmask))


def _kv_kernel(*refs, tm, tk, prompt):
    if prompt:
        (x_ref, g_ref, wk_ref, wv_ref, wf_ref, bf_ref, seg_ref, segt_ref, kg_ref,
         ltri_ref, pneg_ref, ones_ref, kmask_ref,
         k_ref, v_ref, lf_ref, c_ref, kp_ref, vt_ref, carry) = refs
    else:
        (x_ref, g_ref, wk_ref, wv_ref, wf_ref, bf_ref, seg_ref, segt_ref, kg_ref,
         k_ref, v_ref, lf_ref) = refs
    xn = _rms(x_ref[...], g_ref[...])
    xb = xn.astype(bf16)
    k = _head_rms(_dot(xb, wk_ref[...]), seg_ref[...], segt_ref[...], kg_ref[...])
    z = _dot(xb, wf_ref[...]) + bf_ref[...]
    lane = lax.broadcasted_iota(jnp.int32, z.shape, 1)
    logf = jnp.where(lane < 3 * N_HEADS, jax.nn.log_sigmoid(z), 0.0)
    lf_ref[...] = logf
    if not prompt:
        k_ref[...] = k
        v_ref[...] = _dot(xb, wv_ref[...])
        return
    k_ref[...] = k.T
    v_t = _dot_nt(wv_ref[...], xb)
    v_ref[...] = v_t

    @pl.when(pl.program_id(1) == 0)
    def _():
        carry[...] = jnp.zeros(carry.shape, f32)

    c = _dot3_left(ltri_ref[...], logf) + carry[0:1, :]
    carry[...] = jnp.broadcast_to(c[tm - 1:tm, :], carry.shape)
    c2 = c * LOG2E
    c_ref[...] = c2
    hi, mid, lo = _split3(c2)
    pieces = jnp.where(lane < N_HEADS, hi, jnp.where(lane < 2 * N_HEADS, mid, lo))
    aug = _dot(pieces, pneg_ref[...]) + ones_ref[...]
    pairs = []
    for p in range(N_HEADS // 2):
        blk = k[:, p * LANES:(p + 1) * LANES]
        pairs += [blk, blk]
    k_rep = jnp.concatenate(pairs, axis=1)
    kp = jnp.where(kmask_ref[...] > 0.5, k_rep, aug).astype(bf16)
    for h in range(N_HEADS):
        kp_ref[h] = kp[:, h * LANES:(h + 1) * LANES]
    vt = v_t.astype(bf16)
    for j in range(tm // tk):
        vt_ref[j] = vt[:, j * tk:(j + 1) * tk]


def _kv_side(x, g, wk, wv, wf, bfp, kgain, prompt, t_out=None):
    n, t, d = x.shape
    att = N_HEADS * HEAD_DIM
    tm = KV_ROW_TILE if prompt else t
    tk = ATT_TILE
    assert t % tm == 0
    seg, seg_t, pneg, ones_row, kmask = _kv_constants()
    const = lambda shape: pl.BlockSpec(shape, lambda a, i: (0,) * len(shape))
    row = lambda w: pl.BlockSpec((None, tm, w), lambda a, i: (a, i, 0))
    in_specs = [row(d), const((1, d)), const((d, att)), const((d, att)), const((d, LANES)),
                const((1, LANES)), const((att, LANES)), const((LANES, att)), const((1, att))]
    args = [x, g, wk, wv, wf, bfp, seg, seg_t, kgain]
    if prompt:
        assert pl.cdiv(t_out, tm) == t // tm
        kv_spec = pl.BlockSpec((None, att, tm), lambda a, i: (a, 0, i))
        kv_shape = jax.ShapeDtypeStruct((n, att, t_out), f32)
    else:
        kv_spec, kv_shape = row(att), jax.ShapeDtypeStruct((n, t, att), f32)
    out_specs = [kv_spec, kv_spec, row(LANES)]
    out_shape = [kv_shape, kv_shape, jax.ShapeDtypeStruct((n, t, LANES), f32)]
    scratch = []
    if prompt:
        ltri = jnp.asarray(np.tril(np.ones((tm, tm), np.float32)), bf16)
        in_specs += [const((tm, tm)), const((LANES, N_HEADS * LANES)),
                     const((1, N_HEADS * LANES)), const((1, N_HEADS * LANES))]
        args += [ltri, pneg, ones_row, kmask]
        out_specs += [row(LANES), pl.BlockSpec((None, N_HEADS, tm, LANES), lambda a, i: (a, 0, i, 0)),
                      pl.BlockSpec((None, tm // tk, att, tk), lambda a, i: (a, i, 0, 0))]
        out_shape += [jax.ShapeDtypeStruct((n, t, LANES), f32),
                      jax.ShapeDtypeStruct((n, N_HEADS, t, LANES), bf16),
                      jax.ShapeDtypeStruct((n, t // tk, att, tk), bf16)]
        scratch = [pltpu.VMEM((8, LANES), f32)]
    kern = functools.partial(_kv_kernel, tm=tm, tk=tk, prompt=prompt)
    return pl.pallas_call(
        kern,
        grid=(n, t // tm),
        in_specs=in_specs,
        out_specs=out_specs,
        out_shape=out_shape,
        scratch_shapes=scratch,
        compiler_params=_cparams(("arbitrary", "arbitrary")),
        name="kv_prompt" if prompt else "kv_sample",
    )(*args)


def _q_kernel(x_ref, g_ref, wq_ref, seg_ref, segt_ref, qg_ref, o_ref, *, transpose):
    xn = _rms(x_ref[...], g_ref[...])
    q = _head_rms(_dot(xn.astype(bf16), wq_ref[...]), seg_ref[...], segt_ref[...], qg_ref[...])
    if transpose:
        o_ref[...] = q.T.astype(bf16)
    else:
        o_ref[...] = q


def _q_proj(x, g, wq, qgain, layer, j, transpose):
    n, t, d = x.shape
    att = N_HEADS * HEAD_DIM
    tm = MLP_ROW_TILE if t % MLP_ROW_TILE == 0 else t
    seg, seg_t = _kv_constants()[:2]
    const = lambda shape: pl.BlockSpec(shape, lambda a, i: (0,) * len(shape))
    if transpose:
        out_spec = pl.BlockSpec((None, att, tm), lambda a, i: (a, 0, i))
        out_shape = jax.ShapeDtypeStruct((n, att, t), bf16)
    else:
        out_spec = pl.BlockSpec((None, tm, att), lambda a, i: (a, i, 0))
        out_shape = jax.ShapeDtypeStruct((n, t, att), f32)
    return pl.pallas_call(
        functools.partial(_q_kernel, transpose=transpose),
        grid=(n, t // tm),
        in_specs=[
            pl.BlockSpec((None, tm, d), lambda a, i: (a, i, 0)),
            pl.BlockSpec((None, 1, d), lambda a, i: (layer, 0, 0)),
            pl.BlockSpec((None, d, att), lambda a, i: (j, 0, 0)),
            const((att, LANES)), const((LANES, att)),
            pl.BlockSpec((None, 1, att), lambda a, i: (j, 0, 0)),
        ],
        out_specs=out_spec,
        out_shape=out_shape,
        compiler_params=_cparams(("arbitrary", "arbitrary")),
        name="q_proj",
    )(x, g, wq, seg, seg_t, qgain)


def _attn_prompt_kernel(qt_ref, ct_ref, kp_ref, vt_ref, o_ref, rhs_sc, s_sc, m_sc, acc_sc, *, t, hg):
    i = pl.program_id(2)
    row = lax.broadcasted_iota(jnp.int32, (HEAD_DIM, t), 0)
    for e in range(hg):
        hi, mid, lo = _split3(ct_ref[e:e + 1, :])
        aug = jnp.where(row == 0, hi.astype(f32),
                        jnp.where(row == 1, mid.astype(f32),
                                  jnp.where(row == 2, lo.astype(f32),
                                            jnp.where(row < N_AUG, 1.0, 0.0)))).astype(bf16)
        qe = qt_ref[e * HEAD_DIM:(e + 1) * HEAD_DIM, :]
        rhs_sc[e] = jnp.concatenate([qe, aug] if e % 2 == 0 else [aug, qe], axis=0)
        m_sc[e] = jnp.full((1, t), NEG_INF, f32)
        acc_sc[e] = jnp.zeros((HEAD_DIM + 16, t), f32)
    ones_blk = (lax.broadcasted_iota(jnp.int32, (16, t), 0) == 0).astype(bf16)
    causal = lax.broadcasted_iota(jnp.int32, (t, t), 0) <= lax.broadcasted_iota(jnp.int32, (t, t), 1)

    def scores(j, slot):
        off = pl.multiple_of(j * t, t)
        for e in range(hg):
            kb = kp_ref[e, pl.ds(off, t), :]
            s_sc[slot, e] = _dot(kb, rhs_sc[e])

    def update(j, slot, masked):
        ps, alphas = [], []
        for e in range(hg):
            s = s_sc[slot, e]
            if masked:
                s = jnp.where(causal, s, NEG_INF)
            m = m_sc[e]
            m_new = jnp.maximum(m, jnp.max(s, axis=0, keepdims=True))
            ps.append(jnp.exp2(s - m_new).astype(bf16))
            alphas.append(jnp.exp2(m - m_new))
            m_sc[e] = m_new
        for e in range(hg):
            lhs = jnp.concatenate([vt_ref[j, e * HEAD_DIM:(e + 1) * HEAD_DIM, :], ones_blk], axis=0)
            acc_sc[e] = alphas[e] * acc_sc[e] + _dot(lhs, ps[e])

    def scores_upto(j, slot):
        scores(jnp.minimum(j, i), slot)

    scores(0, 0)
    scores_upto(1, 1)
    n_quads = lax.shift_right_logical(i, 2)

    def quad(u, c):
        j = 4 * u
        scores(j + 2, 2)
        scores(j + 3, 3)
        update(j, 0, False)
        update(j + 1, 1, False)
        scores_upto(j + 4, 0)
        scores_upto(j + 5, 1)
        update(j + 2, 2, False)
        update(j + 3, 3, False)
        return c

    lax.fori_loop(0, n_quads, quad, 0)
    rem = i - 4 * n_quads

    @pl.when(rem == 0)
    def _():
        update(i, 0, True)

    @pl.when(rem == 1)
    def _():
        update(i - 1, 0, False)
        update(i, 1, True)

    @pl.when(rem == 2)
    def _():
        scores(i, 2)
        update(i - 2, 0, False)
        update(i - 1, 1, False)
        update(i, 2, True)

    @pl.when(rem == 3)
    def _():
        scores(i - 1, 2)
        scores(i, 3)
        update(i - 3, 0, False)
        update(i - 2, 1, False)
        update(i - 1, 2, False)
        update(i, 3, True)

    outs = []
    for e in range(hg):
        acc = acc_sc[e]
        outs.append(acc[:HEAD_DIM] * pl.reciprocal(acc[HEAD_DIM:HEAD_DIM + 1], approx=False))
    o_ref[...] = jnp.concatenate(outs, axis=0).T.astype(bf16)


def _attn_prompt(qt, ct, kp, vt):
    n, att, tlen = qt.shape
    t, hg = ATT_TILE, HEAD_GROUP
    kern = functools.partial(_attn_prompt_kernel, t=t, hg=hg)
    return pl.pallas_call(
        kern,
        grid=(n, N_HEADS // hg, tlen // t),
        in_specs=[
            pl.BlockSpec((None, hg * HEAD_DIM, t), lambda a, g, i: (a, g, i)),
            pl.BlockSpec((None, None, hg, t), lambda a, g, i: (a, g, 0, i)),
            pl.BlockSpec((None, hg, tlen, LANES), lambda a, g, i: (a, g, 0, 0)),
            pl.BlockSpec((None, tlen // t, hg * HEAD_DIM, t), lambda a, g, i: (a, 0, g, 0)),
        ],
        out_specs=pl.BlockSpec((None, t, hg * HEAD_DIM), lambda a, g, i: (a, i, g)),
        out_shape=jax.ShapeDtypeStruct((n, tlen, att), bf16),
        scratch_shapes=[pltpu.VMEM((hg, LANES, t), bf16), pltpu.VMEM((4, hg, t, t), f32),
                        pltpu.VMEM((hg, 1, t), f32), pltpu.VMEM((hg, HEAD_DIM + 16, t), f32)],
        compiler_params=_cparams(("arbitrary", "arbitrary", "arbitrary")),
        name="attn_prompt",
    )(qt, ct, kp, vt)


def _attn_sample_kernel(*refs, n_steps, pps, page, seq_len):
    pt_ref, q_ref, knew_ref, vnew_ref, lfnew_ref = refs[:5]
    ck_refs = refs[5:5 + pps]
    cv_refs = refs[5 + pps:5 + 2 * pps]
    clf_refs = refs[5 + 2 * pps:5 + 3 * pps]
    usuf_ref, uinc_ref, o_ref, qbd_sc, m_sc, l_sc, acc_sc, carry_sc, e_sc, kpage, vpage = refs[5 + 3 * pps:]
    b = pl.program_id(0)
    j = pl.program_id(1)
    nq = seq_len * N_HEADS
    att = N_HEADS * HEAD_DIM

    def tile_t(x):
        return jnp.concatenate([x] * seq_len, axis=0)

    def softmax_update(s, pv_fn):
        m_prev = m_sc[...]
        m_new = jnp.maximum(m_prev, jnp.max(s, axis=1, keepdims=True))
        p = jnp.exp2(s - m_new)
        alpha = jnp.exp2(m_prev - m_new)
        l_sc[...] = alpha * l_sc[...] + jnp.sum(p, axis=1, keepdims=True)
        acc_sc[...] = alpha * acc_sc[...] + pv_fn(p.astype(bf16))
        m_sc[...] = m_new

    @pl.when(jnp.logical_and(b == 0, j == 0))
    def _():
        kpage[...] = jnp.zeros(kpage.shape, f32)
        vpage[...] = jnp.zeros(vpage.shape, f32)

    @pl.when(j == 0)
    def _():
        q = q_ref[...]
        own = (lax.broadcasted_iota(jnp.int32, (N_HEADS, att), 1) // HEAD_DIM
               == lax.broadcasted_iota(jnp.int32, (N_HEADS, att), 0))
        qbd_sc[...] = jnp.concatenate(
            [jnp.where(own, jnp.broadcast_to(q[t:t + 1], (N_HEADS, att)), 0.0) for t in range(seq_len)],
            axis=0).astype(bf16)
        e_t = _dot3(lfnew_ref[...], uinc_ref[...]) * LOG2E
        e_col = jnp.concatenate([e_t[:, t:t + 1] for t in range(seq_len)], axis=0)
        e_sc[...] = e_col
        carry_sc[...] = jnp.zeros(carry_sc.shape, f32)
        m_sc[...] = jnp.full(m_sc.shape, NEG_INF, f32)
        l_sc[...] = jnp.zeros(l_sc.shape, f32)
        acc_sc[...] = jnp.zeros(acc_sc.shape, f32)
        kpage[0:seq_len, :] = knew_ref[...]
        vpage[0:seq_len, :] = vnew_ref[...]
        s = _dot_nt(qbd_sc[...], kpage[...].astype(bf16))
        lane = lax.broadcasted_iota(jnp.int32, (nq, page), 1)
        t_row = lax.broadcasted_iota(jnp.int32, (nq, page), 0) // N_HEADS
        valid = jnp.logical_and(lane <= t_row, lane < seq_len)
        s = jnp.where(valid, s + (e_col - tile_t(e_t)), NEG_INF)
        softmax_update(s, lambda p: _dot(p, vpage[...].astype(bf16)))

    d_pages = []
    for r in range(pps):
        dd = _dot3(clf_refs[r][...], usuf_ref[...])
        d_pages.append((dd[:, :page] + carry_sc[...]) * LOG2E)
        carry_sc[...] = carry_sc[...] + dd[:, page:]
    kt = jnp.concatenate([ck_refs[r][...].astype(bf16).reshape(att, page) for r in range(pps)], axis=1)
    vt = jnp.concatenate([cv_refs[r][...].astype(bf16).reshape(att, page) for r in range(pps)], axis=1)
    s = _dot(qbd_sc[...], kt) + tile_t(jnp.concatenate(d_pages, axis=1)) + e_sc[...]
    softmax_update(s, lambda p: _dot_nt(p, vt))

    @pl.when(j == n_steps - 1)
    def _():
        o = acc_sc[...] * pl.reciprocal(l_sc[...], approx=False)
        own = (lax.broadcasted_iota(jnp.int32, (nq, att), 1) // HEAD_DIM
               == lax.broadcasted_iota(jnp.int32, (nq, att), 0) % N_HEADS)
        o = jnp.where(own, o, 0.0)
        o_ref[...] = jnp.concatenate(
            [jnp.sum(o[t * N_HEADS:(t + 1) * N_HEADS], axis=0, keepdims=True) for t in range(seq_len)], axis=0)


def _attn_sample(q, k_new, v_new, lf_new_t, cache_kt, cache_vt, cache_lf_t, page_table, seq_len):
    nb = q.shape[0]
    att = N_HEADS * HEAD_DIM
    n_pages = page_table.shape[1]
    page = cache_lf_t.shape[2]
    pps = SAMPLE_PAGES if n_pages % SAMPLE_PAGES == 0 else 1
    n_steps = n_pages // pps
    assert page == LANES
    usuf = np.concatenate([np.triu(np.ones((page, page), np.float32), 1).T, np.ones((page, page), np.float32)], axis=1)
    uinc = np.triu(np.ones((page, page), np.float32))
    pt = page_table.reshape(-1).astype(jnp.int32)
    seq = lambda shape: pl.BlockSpec((None,) + shape, lambda b, j, pt: (b,) + (0,) * len(shape))

    def pool(shape, r):
        return pl.BlockSpec((None,) + shape,
                            lambda b, j, pt: (pt[b * n_pages + n_pages - 1 - (j * pps + r)],) + (0,) * len(shape))

    const = lambda shape: pl.BlockSpec(shape, lambda b, j, pt: (0, 0))
    nq = seq_len * N_HEADS
    kern = functools.partial(_attn_sample_kernel, n_steps=n_steps, pps=pps, page=page, seq_len=seq_len)
    in_specs = [seq((seq_len, att)), seq((seq_len, att)), seq((seq_len, att)), seq((N_HEADS, page))]
    in_specs += [pool((N_HEADS, HEAD_DIM, page), r) for r in range(pps)]
    in_specs += [pool((N_HEADS, HEAD_DIM, page), r) for r in range(pps)]
    in_specs += [pool((N_HEADS, page), r) for r in range(pps)]
    in_specs += [const((page, 2 * page)), const((page, page))]
    return pl.pallas_call(
        kern,
        grid_spec=pltpu.PrefetchScalarGridSpec(
            num_scalar_prefetch=1,
            grid=(nb, n_steps),
            in_specs=in_specs,
            out_specs=seq((seq_len, att)),
            scratch_shapes=[pltpu.VMEM((nq, att), bf16),
                            pltpu.VMEM((nq, 1), f32), pltpu.VMEM((nq, 1), f32), pltpu.VMEM((nq, att), f32),
                            pltpu.VMEM((N_HEADS, page), f32), pltpu.VMEM((nq, 1), f32),
                            pltpu.VMEM((page, att), f32), pltpu.VMEM((page, att), f32)],
        ),
        out_shape=jax.ShapeDtypeStruct((nb, seq_len, att), f32),
        compiler_params=_cparams(("arbitrary", "arbitrary")),
        name="attn_sample",
    )(pt, q, k_new, v_new, lf_new_t, *([cache_kt] * pps), *([cache_vt] * pps), *([cache_lf_t] * pps),
      jnp.asarray(usuf, bf16), jnp.asarray(uinc, bf16))


def kernel(x_prompt, x_sample, cache_k, cache_v, cache_logf, state_conv, page_table, meta_tokens, norm_mix, norm_mlp, conv_w_in, conv_w, conv_w_out, norm_kv, w_kv, k_norm, w_f, b_f, w_q, q_norm, w_o, mlp_w1, mlp_w2):
    nb, seq, d = x_prompt.shape
    db, dseq, _ = x_sample.shape
    att = N_HEADS * HEAD_DIM
    n_a = conv_w_in.shape[0]
    depth = norm_mix.shape[0]
    t_real = seq + N_META
    pad_to = ROW_TILE * ATT_TILE // np.gcd(ROW_TILE, ATT_TILE)
    tp = -(-t_real // pad_to) * pad_to

    g_mix = norm_mix.reshape(depth, 1, d)
    g_mlp = norm_mlp.reshape(depth, 1, d)
    w_in = conv_w_in.astype(bf16)
    w_out = conv_w_out.astype(bf16)
    w1 = mlp_w1.astype(bf16)
    w2 = mlp_w2.astype(bf16)
    wk = w_kv[:, :att].astype(bf16)
    wv = w_kv[:, att:].astype(bf16)
    wv_t = w_kv[:, att:].T.astype(bf16)
    wf = jnp.pad(jnp.tile(w_f, (1, 3)), ((0, 0), (0, LANES - 3 * N_HEADS))).astype(bf16)
    bfp = jnp.pad(jnp.tile(b_f, 3), (0, LANES - 3 * N_HEADS)).reshape(1, LANES)
    g_kv = norm_kv.reshape(1, d)
    kgain = jnp.tile(k_norm, N_HEADS).reshape(1, att)
    wq = w_q.astype(bf16)
    qgain = (jnp.tile(q_norm, (1, N_HEADS)) * (HEAD_DIM ** -0.5 * LOG2E)).reshape(-1, 1, att)
    wo = w_o.astype(bf16)

    meta = jnp.broadcast_to(meta_tokens[None], (nb, N_META, d))
    xp = jnp.concatenate([meta, x_prompt, jnp.zeros((nb, tp - t_real, d), f32)], axis=1)
    conv_rows = []
    for layer in range(n_a):
        xp, st = _conv_prompt(xp, g_mix, w_in, conv_w, w_out, layer, t_real)
        conv_rows.append(st)
        xp = _mlp(xp.reshape(nb * tp, d), g_mlp, w1, w2, layer).reshape(nb, tp, d)
    k_p, v_p, lf_p, c_p, kp, vt = _kv_side(xp, g_kv, wk, wv_t, wf, bfp, kgain, prompt=True, t_out=t_real)
    ct = jnp.swapaxes(c_p[:, :, :N_HEADS], 1, 2).reshape(nb, N_HEADS // HEAD_GROUP, HEAD_GROUP, tp)
    for j in range(depth - n_a):
        layer = n_a + j
        qt = _q_proj(xp, g_mix, wq, qgain, layer, j, transpose=True)
        o = _attn_prompt(qt, ct, kp, vt)
        xp = _mlp(xp.reshape(nb * tp, d), g_mlp, w1, w2, layer,
                  proj=(o.reshape(nb * tp, att), wo, j)).reshape(nb, tp, d)
    y_prompt = xp[:, N_META:t_real]
    k_prompt = jnp.transpose(k_p.reshape(nb, N_HEADS, HEAD_DIM, t_real), (0, 3, 1, 2))
    v_prompt = jnp.transpose(v_p.reshape(nb, N_HEADS, HEAD_DIM, t_real), (0, 3, 1, 2))
    logf_prompt = lf_p[:, :t_real, :N_HEADS]
    conv_prompt = jnp.stack(conv_rows)

    m_s = db * dseq
    xs = x_sample.reshape(m_s, d)
    conv_rows = []
    for layer in range(n_a):
        xs, st = _conv_sample(xs, state_conv[layer], g_mix, w_in, conv_w, w_out, layer, dseq)
        conv_rows.append(st)
        xs = _mlp(xs, g_mlp, w1, w2, layer)
    k_s, v_s, lf_s = _kv_side(xs.reshape(1, m_s, d), g_kv, wk, wv, wf, bfp, kgain, prompt=False)
    page = cache_k.shape[1]
    cache_kt = jnp.transpose(cache_k, (0, 2, 3, 1))
    cache_vt = jnp.transpose(cache_v, (0, 2, 3, 1))
    cache_lf_t = jnp.swapaxes(cache_logf, 1, 2)
    k_new = k_s.reshape(db, dseq, att)
    v_new = v_s.reshape(db, dseq, att)
    lf_new = lf_s.reshape(db, dseq, LANES)[:, :, :N_HEADS]
    lf_new_t = jnp.pad(jnp.swapaxes(lf_new, 1, 2), ((0, 0), (0, 0), (0, page - dseq)))
    for j in range(depth - n_a):
        layer = n_a + j
        q = _q_proj(xs.reshape(1, m_s, d), g_mix, wq, qgain, layer, j, transpose=False)
        o = _attn_sample(q.reshape(db, dseq, att), k_new, v_new, lf_new_t, cache_kt, cache_vt, cache_lf_t,
                         page_table, dseq)
        o = o.reshape(m_s, att).astype(bf16)
        xs = _mlp(xs, g_mlp, w1, w2, layer, proj=(o, wo, j))
    y_sample = xs.reshape(db, dseq, d)
    k_sample = k_s.reshape(db, dseq, N_HEADS, HEAD_DIM)
    v_sample = v_s.reshape(db, dseq, N_HEADS, HEAD_DIM)
    logf_sample = lf_s.reshape(db, dseq, LANES)[:, :, :N_HEADS]
    conv_sample = jnp.stack(conv_rows)

    return (y_prompt, y_sample, k_prompt, v_prompt, logf_prompt, conv_prompt,
            k_sample, v_sample, logf_sample, conv_sample)
```

```python
import functools

import numpy as np
import jax
import jax.numpy as jnp
from jax import lax
from jax.experimental import pallas as pl
from jax.experimental.pallas import tpu as pltpu

N_HEADS = 16
HEAD_DIM = 64
N_META = 16
CONV_WIDTH = 3
RMS_EPS = 1e-6
NEG_INF = -1e30
LOG2E = 1.4426950408889634

LANES = 128
VMEM_LIMIT = 56 * 1024 * 1024
ROW_TILE = 384
MLP_ROW_TILE = 2 * ROW_TILE
ATT_TILE = 256
KV_ROW_TILE = ATT_TILE
HEAD_GROUP = 4
SAMPLE_PAGES = 16
N_AUG = 6

f32 = jnp.float32
bf16 = jnp.bfloat16


def _cparams(sem):
    return pltpu.CompilerParams(dimension_semantics=sem, vmem_limit_bytes=VMEM_LIMIT)


def _dot(a, b):
    return jnp.dot(a, b, preferred_element_type=f32)


def _dot_nt(a, b):
    return lax.dot_general(a, b, (((1,), (1,)), ((), ())), preferred_element_type=f32)


def _rms(x, g):
    ms = jnp.mean(x * x, axis=-1, keepdims=True)
    return x * lax.rsqrt(ms + RMS_EPS) * g


def _split3(x):
    hi = x.astype(bf16)
    r1 = x - hi.astype(f32)
    mid = r1.astype(bf16)
    lo = (r1 - mid.astype(f32)).astype(bf16)
    return hi, mid, lo


def _dot3(x, w):
    hi, mid, lo = _split3(x)
    return _dot(hi, w) + _dot(mid, w) + _dot(lo, w)


def _dot3_left(w, x):
    hi, mid, lo = _split3(x)
    return _dot(w, hi) + _dot(w, mid) + _dot(w, lo)


def _head_rms(y, seg, seg_t, gain):
    ssq = _dot((y * y).astype(bf16), seg)
    inv = lax.rsqrt(ssq * (1.0 / HEAD_DIM) + RMS_EPS)
    hi = inv.astype(bf16)
    lo = (inv - hi.astype(f32)).astype(bf16)
    lane = lax.broadcasted_iota(jnp.int32, inv.shape, 1)
    inv_full = _dot(jnp.where(lane < N_HEADS, hi, lo), seg_t)
    return y * inv_full * gain


def _conv_kernel(*refs, tm, mode, st_tile, st_row, seq_len):
    if mode == "prompt":
        x_ref, g_ref, win_ref, cw_ref, wout_ref, o_ref, st_ref, ubuf = refs
    else:
        x_ref, s1_ref, s2_ref, g_ref, win_ref, cw_ref, wout_ref, o_ref, st_ref = refs
    x = x_ref[...]
    d = x.shape[1]
    xn = _rms(x, g_ref[...])
    bch = _dot(xn.astype(bf16), win_ref[...])
    b = bch[:, :d]
    u = bch[:, d:2 * d] * bch[:, 2 * d:]
    if mode == "prompt":
        i = pl.program_id(1)

        @pl.when(i == 0)
        def _():
            ubuf[0:8, :] = jnp.zeros((8, d), f32)

        @pl.when(i > 0)
        def _():
            ubuf[0:8, :] = ubuf[tm:tm + 8, :]

        ubuf[8:tm + 8, :] = u
        prev1 = ubuf[7:tm + 7, :]
        prev2 = ubuf[6:tm + 6, :]

        @pl.when(i == st_tile)
        def _():
            st_ref[...] = u[st_row:st_row + 8, :]
    else:
        t = lax.broadcasted_iota(jnp.int32, (tm, d), 0) % seq_len
        prev1 = jnp.where(t >= 1, pltpu.roll(u, 1, 0), s1_ref[...])
        prev2 = jnp.where(t >= 2, pltpu.roll(u, 2, 0), s2_ref[...])
        st_ref[...] = u
    cw = cw_ref[...]
    conv = cw[2:3] * u + cw[0:1] * prev2 + cw[1:2] * prev1
    y = _dot((b * conv).astype(bf16), wout_ref[...])
    o_ref[...] = x + y


def _conv_prompt(x, g, w_in, cw, w_out, layer, t_real):
    n, tp, d = x.shape
    tm = MLP_ROW_TILE
    assert tp % tm == 0 and (t_real - 2) % 8 == 6
    st_tile, st_off = divmod(t_real - 2, tm)
    st_row = st_off // 8 * 8
    kern = functools.partial(_conv_kernel, tm=tm, mode="prompt", st_tile=st_tile, st_row=st_row, seq_len=0)
    out, st = pl.pallas_call(
        kern,
        grid=(n, tp // tm),
        in_specs=[
            pl.BlockSpec((None, tm, d), lambda a, i: (a, i, 0)),
            pl.BlockSpec((None, 1, d), lambda a, i: (layer, 0, 0)),
            pl.BlockSpec((None, d, 3 * d), lambda a, i: (layer, 0, 0), pipeline_mode=pl.Buffered(1)),
            pl.BlockSpec((None, CONV_WIDTH, d), lambda a, i: (layer, 0, 0)),
            pl.BlockSpec((None, d, d), lambda a, i: (layer, 0, 0), pipeline_mode=pl.Buffered(1)),
        ],
        out_specs=[
            pl.BlockSpec((None, tm, d), lambda a, i: (a, i, 0)),
            pl.BlockSpec((None, 8, d), lambda a, i: (a, 0, 0)),
        ],
        out_shape=[jax.ShapeDtypeStruct((n, tp, d), f32), jax.ShapeDtypeStruct((n, 8, d), f32)],
        scratch_shapes=[pltpu.VMEM((tm + 8, d), f32)],
        compiler_params=_cparams(("arbitrary", "arbitrary")),
        name="conv_prompt",
    )(x, g, w_in, cw, w_out)
    return out, st[:, 6:8]


def _conv_sample(x, state, g, w_in, cw, w_out, layer, seq_len):
    m, d = x.shape
    nb = m // seq_len
    z = jnp.zeros((nb, seq_len - 1, d), f32)
    s1 = jnp.concatenate([state[:, 1:2], z], axis=1).reshape(m, d)
    s2 = jnp.concatenate([state[:, 0:1], state[:, 1:2], z[:, 1:]], axis=1).reshape(m, d)
    kern = functools.partial(_conv_kernel, tm=m, mode="sample", st_tile=0, st_row=0, seq_len=seq_len)
    full = lambda i: (0, 0)
    out, u = pl.pallas_call(
        kern,
        grid=(1,),
        in_specs=[
            pl.BlockSpec((m, d), full),
            pl.BlockSpec((m, d), full),
            pl.BlockSpec((m, d), full),
            pl.BlockSpec((None, 1, d), lambda i: (layer, 0, 0)),
            pl.BlockSpec((None, d, 3 * d), lambda i: (layer, 0, 0)),
            pl.BlockSpec((None, CONV_WIDTH, d), lambda i: (layer, 0, 0)),
            pl.BlockSpec((None, d, d), lambda i: (layer, 0, 0)),
        ],
        out_specs=[pl.BlockSpec((m, d), full), pl.BlockSpec((m, d), full)],
        out_shape=[jax.ShapeDtypeStruct((m, d), f32), jax.ShapeDtypeStruct((m, d), f32)],
        compiler_params=_cparams(("arbitrary",)),
        name="conv_sample",
    )(x, s1, s2, g, w_in, cw, w_out)
    return out, u.reshape(nb, seq_len, d)[:, seq_len - 2:]


def _mlp_kernel(*refs, has_proj, n_chunks):
    if has_proj:
        x_ref, a_ref, wa_ref, g_ref, w1_ref, w2_ref, o_ref = refs
    else:
        x_ref, g_ref, w1_ref, w2_ref, o_ref = refs
    x = x_ref[...]
    if has_proj:
        x = x + _dot(a_ref[...], wa_ref[...])
    xn = _rms(x, g_ref[...]).astype(bf16)
    ck = w1_ref.shape[1] // n_chunks
    y = x
    for c in range(n_chunks):
        h = _dot(xn, w1_ref[:, c * ck:(c + 1) * ck])
        h = jnp.square(jnp.maximum(h, 0.0)).astype(bf16)
        y = y + _dot(h, w2_ref[c * ck:(c + 1) * ck, :])
    o_ref[...] = y


def _mlp(x, g, w1, w2, layer, proj=None):
    m, d = x.shape
    dff = w1.shape[2]
    tm = MLP_ROW_TILE if m % MLP_ROW_TILE == 0 else m
    row = lambda i: (i, 0)
    in_specs = [pl.BlockSpec((tm, d), row)]
    args = [x]
    if proj is not None:
        a, wa, pl_idx = proj
        in_specs += [pl.BlockSpec((tm, a.shape[1]), row),
                     pl.BlockSpec((None, a.shape[1], d), lambda i: (pl_idx, 0, 0))]
        args += [a, wa]
    in_specs += [
        pl.BlockSpec((None, 1, d), lambda i: (layer, 0, 0)),
        pl.BlockSpec((None, d, dff), lambda i: (layer, 0, 0)),
        pl.BlockSpec((None, dff, d), lambda i: (layer, 0, 0)),
    ]
    args += [g, w1, w2]
    kern = functools.partial(_mlp_kernel, has_proj=proj is not None, n_chunks=4)
    return pl.pallas_call(
        kern,
        grid=(m // tm,),
        in_specs=in_specs,
        out_specs=pl.BlockSpec((tm, d), row),
        out_shape=jax.ShapeDtypeStruct((m, d), f32),
        compiler_params=_cparams(("arbitrary",)),
        name="mlp",
    )(*args)


def _aug_offset(h):
    return LANES * h + (1 - h % 2) * HEAD_DIM


def _kv_constants():
    att = N_HEADS * HEAD_DIM
    seg = np.zeros((att, LANES), np.float32)
    seg[np.arange(att), np.arange(att) // HEAD_DIM] = 1.0
    seg[np.arange(att), N_HEADS + np.arange(att) // HEAD_DIM] = 1.0
    pneg = np.zeros((LANES, N_HEADS * LANES), np.float32)
    ones_row = np.zeros((1, N_HEADS * LANES), np.float32)
    kmask = np.zeros((1, N_HEADS * LANES), np.float32)
    for h in range(N_HEADS):
        a = _aug_offset(h)
        ones_row[0, a:a + 3] = 1.0
        for part in range(3):
            pneg[part * N_HEADS + h, a + 3 + part] = -1.0
        k0 = LANES * h + (h % 2) * HEAD_DIM
        kmask[0, k0:k0 + HEAD_DIM] = 1.0
    return (jnp.asarray(seg, bf16), jnp.asarray(seg.T, bf16), jnp.asarray(pneg, bf16),
            jnp.asarray(ones_row), jnp.asarray(kmask))


def _kv_kernel(*refs, tm, tk, prompt):
    if prompt:
        (x_ref, g_ref, wk_ref, wv_ref, wf_ref, bf_ref, seg_ref, segt_ref, kg_ref,
         ltri_ref, pneg_ref, ones_ref, kmask_ref,
         k_ref, v_ref, lf_ref, c_ref, kp_ref, vt_ref, carry) = refs
    else:
        (x_ref, g_ref, wk_ref, wv_ref, wf_ref, bf_ref, seg_ref, segt_ref, kg_ref,
         k_ref, v_ref, lf_ref) = refs
    xn = _rms(x_ref[...], g_ref[...])
    xb = xn.astype(bf16)
    k = _head_rms(_dot(xb, wk_ref[...]), seg_ref[...], segt_ref[...], kg_ref[...])
    z = _dot(xb, wf_ref[...]) + bf_ref[...]
    lane = lax.broadcasted_iota(jnp.int32, z.shape, 1)
    logf = jnp.where(lane < 3 * N_HEADS, jax.nn.log_sigmoid(z), 0.0)
    lf_ref[...] = logf
    if not prompt:
        k_ref[...] = k
        v_ref[...] = _dot(xb, wv_ref[...])
        return
    k_ref[...] = k.T
    v_t = _dot_nt(wv_ref[...], xb)
    v_ref[...] = v_t

    @pl.when(pl.program_id(1) == 0)
    def _():
        carry[...] = jnp.zeros(carry.shape, f32)

    c = _dot3_left(ltri_ref[...], logf) + carry[0:1, :]
    carry[...] = jnp.broadcast_to(c[tm - 1:tm, :], carry.shape)
    c2 = c * LOG2E
    c_ref[...] = c2
    hi, mid, lo = _split3(c2)
    pieces = jnp.where(lane < N_HEADS, hi, jnp.where(lane < 2 * N_HEADS, mid, lo))
    aug = _dot(pieces, pneg_ref[...]) + ones_ref[...]
    pairs = []
    for p in range(N_HEADS // 2):
        blk = k[:, p * LANES:(p + 1) * LANES]
        pairs += [blk, blk]
    k_rep = jnp.concatenate(pairs, axis=1)
    kp = jnp.where(kmask_ref[...] > 0.5, k_rep, aug).astype(bf16)
    for h in range(N_HEADS):
        kp_ref[h] = kp[:, h * LANES:(h + 1) * LANES]
    vt = v_t.astype(bf16)
    for j in range(tm // tk):
        vt_ref[j] = vt[:, j * tk:(j + 1) * tk]


def _kv_side(x, g, wk, wv, wf, bfp, kgain, prompt, t_out=None):
    n, t, d = x.shape
    att = N_HEADS * HEAD_DIM
    tm = KV_ROW_TILE if prompt else t
    tk = ATT_TILE
    assert t % tm == 0
    seg, seg_t, pneg, ones_row, kmask = _kv_constants()
    const = lambda shape: pl.BlockSpec(shape, lambda a, i: (0,) * len(shape))
    row = lambda w: pl.BlockSpec((None, tm, w), lambda a, i: (a, i, 0))
    in_specs = [row(d), const((1, d)), const((d, att)), const((d, att)), const((d, LANES)),
                const((1, LANES)), const((att, LANES)), const((LANES, att)), const((1, att))]
    args = [x, g, wk, wv, wf, bfp, seg, seg_t, kgain]
    if prompt:
        assert pl.cdiv(t_out, tm) == t // tm
        kv_spec = pl.BlockSpec((None, att, tm), lambda a, i: (a, 0, i))
        kv_shape = jax.ShapeDtypeStruct((n, att, t_out), f32)
    else:
        kv_spec, kv_shape = row(att), jax.ShapeDtypeStruct((n, t, att), f32)
    out_specs = [kv_spec, kv_spec, row(LANES)]
    out_shape = [kv_shape, kv_shape, jax.ShapeDtypeStruct((n, t, LANES), f32)]
    scratch = []
    if prompt:
        ltri = jnp.asarray(np.tril(np.ones((tm, tm), np.float32)), bf16)
        in_specs += [const((tm, tm)), const((LANES, N_HEADS * LANES)),
                     const((1, N_HEADS * LANES)), const((1, N_HEADS * LANES))]
        args += [ltri, pneg, ones_row, kmask]
        out_specs += [row(LANES), pl.BlockSpec((None, N_HEADS, tm, LANES), lambda a, i: (a, 0, i, 0)),
                      pl.BlockSpec((None, tm // tk, att, tk), lambda a, i: (a, i, 0, 0))]
        out_shape += [jax.ShapeDtypeStruct((n, t, LANES), f32),
                      jax.ShapeDtypeStruct((n, N_HEADS, t, LANES), bf16),
                      jax.ShapeDtypeStruct((n, t // tk, att, tk), bf16)]
        scratch = [pltpu.VMEM((8, LANES), f32)]
    kern = functools.partial(_kv_kernel, tm=tm, tk=tk, prompt=prompt)
    return pl.pallas_call(
        kern,
        grid=(n, t // tm),
        in_specs=in_specs,
        out_specs=out_specs,
        out_shape=out_shape,
        scratch_shapes=scratch,
        compiler_params=_cparams(("arbitrary", "arbitrary")),
        name="kv_prompt" if prompt else "kv_sample",
    )(*args)


def _q_kernel(x_ref, g_ref, wq_ref, seg_ref, segt_ref, qg_ref, o_ref, *, transpose):
    xn = _rms(x_ref[...], g_ref[...])
    q = _head_rms(_dot(xn.astype(bf16), wq_ref[...]), seg_ref[...], segt_ref[...], qg_ref[...])
    if transpose:
        o_ref[...] = q.T.astype(bf16)
    else:
        o_ref[...] = q


def _q_proj(x, g, wq, qgain, layer, j, transpose):
    n, t, d = x.shape
    att = N_HEADS * HEAD_DIM
    tm = MLP_ROW_TILE if t % MLP_ROW_TILE == 0 else t
    seg, seg_t = _kv_constants()[:2]
    const = lambda shape: pl.BlockSpec(shape, lambda a, i: (0,) * len(shape))
    if transpose:
        out_spec = pl.BlockSpec((None, att, tm), lambda a, i: (a, 0, i))
        out_shape = jax.ShapeDtypeStruct((n, att, t), bf16)
    else:
        out_spec = pl.BlockSpec((None, tm, att), lambda a, i: (a, i, 0))
        out_shape = jax.ShapeDtypeStruct((n, t, att), f32)
    return pl.pallas_call(
        functools.partial(_q_kernel, transpose=transpose),
        grid=(n, t // tm),
        in_specs=[
            pl.BlockSpec((None, tm, d), lambda a, i: (a, i, 0)),
            pl.BlockSpec((None, 1, d), lambda a, i: (layer, 0, 0)),
            pl.BlockSpec((None, d, att), lambda a, i: (j, 0, 0)),
            const((att, LANES)), const((LANES, att)),
            pl.BlockSpec((None, 1, att), lambda a, i: (j, 0, 0)),
        ],
        out_specs=out_spec,
        out_shape=out_shape,
        compiler_params=_cparams(("arbitrary", "arbitrary")),
        name="q_proj",
    )(x, g, wq, seg, seg_t, qgain)


def _attn_prompt_kernel(qt_ref, ct_ref, kp_ref, vt_ref, o_ref, rhs_sc, s_sc, m_sc, acc_sc, *, t, hg):
    i = pl.program_id(2)
    row = lax.broadcasted_iota(jnp.int32, (HEAD_DIM, t), 0)
    for e in range(hg):
        hi, mid, lo = _split3(ct_ref[e:e + 1, :])
        aug = jnp.where(row == 0, hi.astype(f32),
                        jnp.where(row == 1, mid.astype(f32),
                                  jnp.where(row == 2, lo.astype(f32),
                                            jnp.where(row < N_AUG, 1.0, 0.0)))).astype(bf16)
        qe = qt_ref[e * HEAD_DIM:(e + 1) * HEAD_DIM, :]
        rhs_sc[e] = jnp.concatenate([qe, aug] if e % 2 == 0 else [aug, qe], axis=0)
        m_sc[e] = jnp.full((1, t), NEG_INF, f32)
        acc_sc[e] = jnp.zeros((HEAD_DIM + 16, t), f32)
    ones_blk = (lax.broadcasted_iota(jnp.int32, (16, t), 0) == 0).astype(bf16)
    causal = lax.broadcasted_iota(jnp.int32, (t, t), 0) <= lax.broadcasted_iota(jnp.int32, (t, t), 1)

    def scores(j, slot):
        off = pl.multiple_of(j * t, t)
        for e in range(hg):
            kb = kp_ref[e, pl.ds(off, t), :]
            s_sc[slot, e] = _dot(kb, rhs_sc[e])

    def update(j, slot, masked):
        ps, alphas = [], []
        for e in range(hg):
            s = s_sc[slot, e]
            if masked:
                s = jnp.where(causal, s, NEG_INF)
            m = m_sc[e]
            m_new = jnp.maximum(m, jnp.max(s, axis=0, keepdims=True))
            ps.append(jnp.exp2(s - m_new).astype(bf16))
            alphas.append(jnp.exp2(m - m_new))
            m_sc[e] = m_new
        for e in range(hg):
            lhs = jnp.concatenate([vt_ref[j, e * HEAD_DIM:(e + 1) * HEAD_DIM, :], ones_blk], axis=0)
            acc_sc[e] = alphas[e] * acc_sc[e] + _dot(lhs, ps[e])

    def scores_upto(j, slot):
        scores(jnp.minimum(j, i), slot)

    scores(0, 0)
    scores_upto(1, 1)
    n_quads = lax.shift_right_logical(i, 2)

    def quad(u, c):
        j = 4 * u
        scores(j + 2, 2)
        scores(j + 3, 3)
        update(j, 0, False)
        update(j + 1, 1, False)
        scores_upto(j + 4, 0)
        scores_upto(j + 5, 1)
        update(j + 2, 2, False)
        update(j + 3, 3, False)
        return c

    lax.fori_loop(0, n_quads, quad, 0)
    rem = i - 4 * n_quads

    @pl.when(rem == 0)
    def _():
        update(i, 0, True)

    @pl.when(rem == 1)
    def _():
        update(i - 1, 0, False)
        update(i, 1, True)

    @pl.when(rem == 2)
    def _():
        scores(i, 2)
        update(i - 2, 0, False)
        update(i - 1, 1, False)
        update(i, 2, True)

    @pl.when(rem == 3)
    def _():
        scores(i - 1, 2)
        scores(i, 3)
        update(i - 3, 0, False)
        update(i - 2, 1, False)
        update(i - 1, 2, False)
        update(i, 3, True)

    outs = []
    for e in range(hg):
        acc = acc_sc[e]
        outs.append(acc[:HEAD_DIM] * pl.reciprocal(acc[HEAD_DIM:HEAD_DIM + 1], approx=False))
    o_ref[...] = jnp.concatenate(outs, axis=0).T.astype(bf16)


def _attn_prompt(qt, ct, kp, vt):
    n, att, tlen = qt.shape
    t, hg = ATT_TILE, HEAD_GROUP
    kern = functools.partial(_attn_prompt_kernel, t=t, hg=hg)
    return pl.pallas_call(
        kern,
        grid=(n, N_HEADS // hg, tlen // t),
        in_specs=[
            pl.BlockSpec((None, hg * HEAD_DIM, t), lambda a, g, i: (a, g, i)),
            pl.BlockSpec((None, None, hg, t), lambda a, g, i: (a, g, 0, i)),
            pl.BlockSpec((None, hg, tlen, LANES), lambda a, g, i: (a, g, 0, 0)),
            pl.BlockSpec((None, tlen // t, hg * HEAD_DIM, t), lambda a, g, i: (a, 0, g, 0)),
        ],
        out_specs=pl.BlockSpec((None, t, hg * HEAD_DIM), lambda a, g, i: (a, i, g)),
        out_shape=jax.ShapeDtypeStruct((n, tlen, att), bf16),
        scratch_shapes=[pltpu.VMEM((hg, LANES, t), bf16), pltpu.VMEM((4, hg, t, t), f32),
                        pltpu.VMEM((hg, 1, t), f32), pltpu.VMEM((hg, HEAD_DIM + 16, t), f32)],
        compiler_params=_cparams(("arbitrary", "arbitrary", "arbitrary")),
        name="attn_prompt",
    )(qt, ct, kp, vt)


def _attn_sample_kernel(*refs, n_steps, pps, page, seq_len):
    pt_ref, q_ref, knew_ref, vnew_ref, lfnew_ref = refs[:5]
    ck_refs = refs[5:5 + pps]
    cv_refs = refs[5 + pps:5 + 2 * pps]
    clf_refs = refs[5 + 2 * pps:5 + 3 * pps]
    usuf_ref, uinc_ref, o_ref, qbd_sc, m_sc, l_sc, acc_sc, carry_sc, e_sc, kpage, vpage = refs[5 + 3 * pps:]
    b = pl.program_id(0)
    j = pl.program_id(1)
    nq = seq_len * N_HEADS
    att = N_HEADS * HEAD_DIM

    def tile_t(x):
        return jnp.concatenate([x] * seq_len, axis=0)

    def softmax_update(s, pv_fn):
        m_prev = m_sc[...]
        m_new = jnp.maximum(m_prev, jnp.max(s, axis=1, keepdims=True))
        p = jnp.exp2(s - m_new)
        alpha = jnp.exp2(m_prev - m_new)
        l_sc[...] = alpha * l_sc[...] + jnp.sum(p, axis=1, keepdims=True)
        acc_sc[...] = alpha * acc_sc[...] + pv_fn(p.astype(bf16))
        m_sc[...] = m_new

    @pl.when(jnp.logical_and(b == 0, j == 0))
    def _():
        kpage[...] = jnp.zeros(kpage.shape, f32)
        vpage[...] = jnp.zeros(vpage.shape, f32)

    @pl.when(j == 0)
    def _():
        q = q_ref[...]
        own = (lax.broadcasted_iota(jnp.int32, (N_HEADS, att), 1) // HEAD_DIM
               == lax.broadcasted_iota(jnp.int32, (N_HEADS, att), 0))
        qbd_sc[...] = jnp.concatenate(
            [jnp.where(own, jnp.broadcast_to(q[t:t + 1], (N_HEADS, att)), 0.0) for t in range(seq_len)],
            axis=0).astype(bf16)
        e_t = _dot3(lfnew_ref[...], uinc_ref[...]) * LOG2E
        e_col = jnp.concatenate([e_t[:, t:t + 1] for t in range(seq_len)], axis=0)
        e_sc[...] = e_col
        carry_sc[...] = jnp.zeros(carry_sc.shape, f32)
        m_sc[...] = jnp.full(m_sc.shape, NEG_INF, f32)
        l_sc[...] = jnp.zeros(l_sc.shape, f32)
        acc_sc[...] = jnp.zeros(acc_sc.shape, f32)
        kpage[0:seq_len, :] = knew_ref[...]
        vpage[0:seq_len, :] = vnew_ref[...]
        s = _dot_nt(qbd_sc[...], kpage[...].astype(bf16))
        lane = lax.broadcasted_iota(jnp.int32, (nq, page), 1)
        t_row = lax.broadcasted_iota(jnp.int32, (nq, page), 0) // N_HEADS
        valid = jnp.logical_and(lane <= t_row, lane < seq_len)
        s = jnp.where(valid, s + (e_col - tile_t(e_t)), NEG_INF)
        softmax_update(s, lambda p: _dot(p, vpage[...].astype(bf16)))

    d_pages = []
    for r in range(pps):
        dd = _dot3(clf_refs[r][...], usuf_ref[...])
        d_pages.append((dd[:, :page] + carry_sc[...]) * LOG2E)
        carry_sc[...] = carry_sc[...] + dd[:, page:]
    kt = jnp.concatenate([ck_refs[r][...].astype(bf16).reshape(att, page) for r in range(pps)], axis=1)
    vt = jnp.concatenate([cv_refs[r][...].astype(bf16).reshape(att, page) for r in range(pps)], axis=1)
    s = _dot(qbd_sc[...], kt) + tile_t(jnp.concatenate(d_pages, axis=1)) + e_sc[...]
    softmax_update(s, lambda p: _dot_nt(p, vt))

    @pl.when(j == n_steps - 1)
    def _():
        o = acc_sc[...] * pl.reciprocal(l_sc[...], approx=False)
        own = (lax.broadcasted_iota(jnp.int32, (nq, att), 1) // HEAD_DIM
               == lax.broadcasted_iota(jnp.int32, (nq, att), 0) % N_HEADS)
        o = jnp.where(own, o, 0.0)
        o_ref[...] = jnp.concatenate(
            [jnp.sum(o[t * N_HEADS:(t + 1) * N_HEADS], axis=0, keepdims=True) for t in range(seq_len)], axis=0)


def _attn_sample(q, k_new, v_new, lf_new_t, cache_kt, cache_vt, cache_lf_t, page_table, seq_len):
    nb = q.shape[0]
    att = N_HEADS * HEAD_DIM
    n_pages = page_table.shape[1]
    page = cache_lf_t.shape[2]
    pps = SAMPLE_PAGES if n_pages % SAMPLE_PAGES == 0 else 1
    n_steps = n_pages // pps
    assert page == LANES
    usuf = np.concatenate([np.triu(np.ones((page, page), np.float32), 1).T, np.ones((page, page), np.float32)], axis=1)
    uinc = np.triu(np.ones((page, page), np.float32))
    pt = page_table.reshape(-1).astype(jnp.int32)
    seq = lambda shape: pl.BlockSpec((None,) + shape, lambda b, j, pt: (b,) + (0,) * len(shape))

    def pool(shape, r):
        return pl.BlockSpec((None,) + shape,
                            lambda b, j, pt: (pt[b * n_pages + n_pages - 1 - (j * pps + r)],) + (0,) * len(shape))

    const = lambda shape: pl.BlockSpec(shape, lambda b, j, pt: (0, 0))
    nq = seq_len * N_HEADS
    kern = functools.partial(_attn_sample_kernel, n_steps=n_steps, pps=pps, page=page, seq_len=seq_len)
    in_specs = [seq((seq_len, att)), seq((seq_len, att)), seq((seq_len, att)), seq((N_HEADS, page))]
    in_specs += [pool((N_HEADS, HEAD_DIM, page), r) for r in range(pps)]
    in_specs += [pool((N_HEADS, HEAD_DIM, page), r) for r in range(pps)]
    in_specs += [pool((N_HEADS, page), r) for r in range(pps)]
    in_specs += [const((page, 2 * page)), const((page, page))]
    return pl.pallas_call(
        kern,
        grid_spec=pltpu.PrefetchScalarGridSpec(
            num_scalar_prefetch=1,
            grid=(nb, n_steps),
            in_specs=in_specs,
            out_specs=seq((seq_len, att)),
            scratch_shapes=[pltpu.VMEM((nq, att), bf16),
                            pltpu.VMEM((nq, 1), f32), pltpu.VMEM((nq, 1), f32), pltpu.VMEM((nq, att), f32),
                            pltpu.VMEM((N_HEADS, page), f32), pltpu.VMEM((nq, 1), f32),
                            pltpu.VMEM((page, att), f32), pltpu.VMEM((page, att), f32)],
        ),
        out_shape=jax.ShapeDtypeStruct((nb, seq_len, att), f32),
        compiler_params=_cparams(("arbitrary", "arbitrary")),
        name="attn_sample",
    )(pt, q, k_new, v_new, lf_new_t, *([cache_kt] * pps), *([cache_vt] * pps), *([cache_lf_t] * pps),
      jnp.asarray(usuf, bf16), jnp.asarray(uinc, bf16))


def kernel(x_prompt, x_sample, cache_k, cache_v, cache_logf, state_conv, page_table, meta_tokens, norm_mix, norm_mlp, conv_w_in, conv_w, conv_w_out, norm_kv, w_kv, k_norm, w_f, b_f, w_q, q_norm, w_o, mlp_w1, mlp_w2):
    nb, seq, d = x_prompt.shape
    db, dseq, _ = x_sample.shape
    att = N_HEADS * HEAD_DIM
    n_a = conv_w_in.shape[0]
    depth = norm_mix.shape[0]
    t_real = seq + N_META
    pad_to = ROW_TILE * ATT_TILE // np.gcd(ROW_TILE, ATT_TILE)
    tp = -(-t_real // pad_to) * pad_to

    g_mix = norm_mix.reshape(depth, 1, d)
    g_mlp = norm_mlp.reshape(depth, 1, d)
    w_in = conv_w_in.astype(bf16)
    w_out = conv_w_out.astype(bf16)
    w1 = mlp_w1.astype(bf16)
    w2 = mlp_w2.astype(bf16)
    wk = w_kv[:, :att].astype(bf16)
    wv = w_kv[:, att:].astype(bf16)
    wv_t = w_kv[:, att:].T.astype(bf16)
    wf = jnp.pad(jnp.tile(w_f, (1, 3)), ((0, 0), (0, LANES - 3 * N_HEADS))).astype(bf16)
    bfp = jnp.pad(jnp.tile(b_f, 3), (0, LANES - 3 * N_HEADS)).reshape(1, LANES)
    g_kv = norm_kv.reshape(1, d)
    kgain = jnp.tile(k_norm, N_HEADS).reshape(1, att)
    wq = w_q.astype(bf16)
    qgain = (jnp.tile(q_norm, (1, N_HEADS)) * (HEAD_DIM ** -0.5 * LOG2E)).reshape(-1, 1, att)
    wo = w_o.astype(bf16)

    meta = jnp.broadcast_to(meta_tokens[None], (nb, N_META, d))
    xp = jnp.concatenate([meta, x_prompt, jnp.zeros((nb, tp - t_real, d), f32)], axis=1)
    conv_rows = []
    for layer in range(n_a):
        xp, st = _conv_prompt(xp, g_mix, w_in, conv_w, w_out, layer, t_real)
        conv_rows.append(st)
        xp = _mlp(xp.reshape(nb * tp, d), g_mlp, w1, w2, layer).reshape(nb, tp, d)
    k_p, v_p, lf_p, c_p, kp, vt = _kv_side(xp, g_kv, wk, wv_t, wf, bfp, kgain, prompt=True, t_out=t_real)
    ct = jnp.swapaxes(c_p[:, :, :N_HEADS], 1, 2).reshape(nb, N_HEADS // HEAD_GROUP, HEAD_GROUP, tp)
    for j in range(depth - n_a):
        layer = n_a + j
        qt = _q_proj(xp, g_mix, wq, qgain, layer, j, transpose=True)
        o = _attn_prompt(qt, ct, kp, vt)
        xp = _mlp(xp.reshape(nb * tp, d), g_mlp, w1, w2, layer,
                  proj=(o.reshape(nb * tp, att), wo, j)).reshape(nb, tp, d)
    y_prompt = xp[:, N_META:t_real]
    k_prompt = jnp.transpose(k_p.reshape(nb, N_HEADS, HEAD_DIM, t_real), (0, 3, 1, 2))
    v_prompt = jnp.transpose(v_p.reshape(nb, N_HEADS, HEAD_DIM, t_real), (0, 3, 1, 2))
    logf_prompt = lf_p[:, :t_real, :N_HEADS]
    conv_prompt = jnp.stack(conv_rows)

    m_s = db * dseq
    xs = x_sample.reshape(m_s, d)
    conv_rows = []
    for layer in range(n_a):
        xs, st = _conv_sample(xs, state_conv[layer], g_mix, w_in, conv_w, w_out, layer, dseq)
        conv_rows.append(st)
        xs = _mlp(xs, g_mlp, w1, w2, layer)
    k_s, v_s, lf_s = _kv_side(xs.reshape(1, m_s, d), g_kv, wk, wv, wf, bfp, kgain, prompt=False)
    page = cache_k.shape[1]
    cache_kt = jnp.transpose(cache_k, (0, 2, 3, 1))
    cache_vt = jnp.transpose(cache_v, (0, 2, 3, 1))
    cache_lf_t = jnp.swapaxes(cache_logf, 1, 2)
    k_new = k_s.reshape(db, dseq, att)
    v_new = v_s.reshape(db, dseq, att)
    lf_new = lf_s.reshape(db, dseq, LANES)[:, :, :N_HEADS]
    lf_new_t = jnp.pad(jnp.swapaxes(lf_new, 1, 2), ((0, 0), (0, 0), (0, page - dseq)))
    for j in range(depth - n_a):
        layer = n_a + j
        q = _q_proj(xs.reshape(1, m_s, d), g_mix, wq, qgain, layer, j, transpose=False)
        o = _attn_sample(q.reshape(db, dseq, att), k_new, v_new, lf_new_t, cache_kt, cache_vt, cache_lf_t,
                         page_table, dseq)
        o = o.reshape(m_s, att).astype(bf16)
        xs = _mlp(xs, g_mlp, w1, w2, layer, proj=(o, wo, j))
    y_sample = xs.reshape(db, dseq, d)
    k_sample = k_s.reshape(db, dseq, N_HEADS, HEAD_DIM)
    v_sample = v_s.reshape(db, dseq, N_HEADS, HEAD_DIM)
    logf_sample = lf_s.reshape(db, dseq, LANES)[:, :, :N_HEADS]
    conv_sample = jnp.stack(conv_rows)

    return (y_prompt, y_sample, k_prompt, v_prompt, logf_prompt, conv_prompt,
            k_sample, v_sample, logf_sample, conv_sample)
```
